```python
import jax, jax.numpy as jnp
from jax import lax
import numpy as np

D_MODEL = 1024
BATCH = 4
SEQ = 4096
DEPTH = 4
DEC_BATCH = 128
DEC_SEQ = 4
PAST_LEN = 8192
PAGE_SIZE = 128

N_MIXERS = 3
HEAD_DIM = 64
SWA_Q_HEADS = 16
SWA_KV_HEADS = 4
SWA_GROUP = SWA_Q_HEADS // SWA_KV_HEADS
WINDOW = 128
FOX_HEADS = 16
FOX_BLOCK = 128
CONV_WIDTH = 3
FFN_DIM = ((8 * D_MODEL // 3 + 255) // 256) * 256
PLE_DIM = 256
ROPE_THETA = 10000.0
EPS = 1e-6
NEG = -1e30
N_SWA = (DEPTH + 2) // 3
N_FOX = (DEPTH + 1) // 3
N_CONV = DEPTH // 3

kernel_name = "hybrid_swa_fox_shortconv_decoder_step"


def rmsnorm(x, g):
    xf = x.astype(jnp.float32)
    y = xf * lax.rsqrt(jnp.mean(xf * xf, axis=-1, keepdims=True) + EPS)
    return (y * g.astype(jnp.float32)).astype(x.dtype)


def rope(x, pos):
    half = HEAD_DIM // 2
    inv = ROPE_THETA ** (-jnp.arange(half, dtype=jnp.float32) / half)
    ang = pos.astype(jnp.float32)[:, None] * inv[None, :]
    cos = jnp.cos(ang)[:, None, :]
    sin = jnp.sin(ang)[:, None, :]
    x1 = x[..., :half].astype(jnp.float32)
    x2 = x[..., half:].astype(jnp.float32)
    out = jnp.concatenate([x1 * cos - x2 * sin, x2 * cos + x1 * sin], axis=-1)
    return out.astype(x.dtype)


def swa_project(xn, w_qkv, b_qkv, pos):
    lead = xn.shape[:-1]
    qkv = xn @ w_qkv + b_qkv
    q, k, v = jnp.split(qkv, [SWA_Q_HEADS * HEAD_DIM, (SWA_Q_HEADS + SWA_KV_HEADS) * HEAD_DIM], axis=-1)
    q = rope(q.reshape(*lead, SWA_Q_HEADS, HEAD_DIM), pos)
    q = q.reshape(*lead, SWA_KV_HEADS, SWA_GROUP, HEAD_DIM)
    k = rope(k.reshape(*lead, SWA_KV_HEADS, HEAD_DIM), pos)
    v = v.reshape(*lead, SWA_KV_HEADS, HEAD_DIM)
    return q, k, v


def sink_attend(q, k, v, mask, sink):
    s = jnp.einsum('...qkgd,...skd->...kgqs', q, k).astype(jnp.float32) * (HEAD_DIM ** -0.5)
    s = jnp.where(mask, s, NEG)
    sk = sink.astype(jnp.float32).reshape(SWA_KV_HEADS, SWA_GROUP)[:, :, None, None]
    m = jnp.maximum(jnp.max(s, axis=-1, keepdims=True), sk)
    e = jnp.exp(s - m)
    den = jnp.sum(e, axis=-1, keepdims=True) + jnp.exp(sk - m)
    p = (e / den).astype(v.dtype)
    return jnp.einsum('...kgqs,...skd->...qkgd', p, v)


def swa_prompt(xn, w_qkv, b_qkv, w_o, b_o, sink):
    B, S, _ = xn.shape
    q, k, v = swa_project(xn, w_qkv, b_qkv, jnp.arange(S))
    nb = S // WINDOW
    qb = q.reshape(B, nb, WINDOW, SWA_KV_HEADS, SWA_GROUP, HEAD_DIM)
    kb = k.reshape(B, nb, WINDOW, SWA_KV_HEADS, HEAD_DIM)
    vb = v.reshape(B, nb, WINDOW, SWA_KV_HEADS, HEAD_DIM)
    padw = ((0, 0), (1, 0), (0, 0), (0, 0), (0, 0))
    kband = jnp.concatenate([jnp.pad(kb[:, :-1], padw), kb], axis=2)
    vband = jnp.concatenate([jnp.pad(vb[:, :-1], padw), vb], axis=2)
    qi = jnp.arange(WINDOW)[:, None]
    kj = jnp.arange(2 * WINDOW)[None, :]
    d = qi + WINDOW - kj
    first = (jnp.arange(nb) == 0)[:, None, None]
    mask = (d >= 0) & (d < WINDOW) & ~(first & (kj < WINDOW))
    o = sink_attend(qb, kband, vband, mask[None, :, None, None], sink)
    y = o.reshape(B, S, SWA_Q_HEADS * HEAD_DIM) @ w_o + b_o
    buf = min(WINDOW, S)
    return y, k[:, S - buf:], v[:, S - buf:]


def swa_sample(xn, cache_k, cache_v, w_qkv, b_qkv, w_o, b_o, sink):
    DB, T, _ = xn.shape
    buf = cache_k.shape[1]
    qpos = PAST_LEN + jnp.arange(T)
    q, k, v = swa_project(xn, w_qkv, b_qkv, qpos)
    kall = jnp.concatenate([cache_k, k], axis=1)
    vall = jnp.concatenate([cache_v, v], axis=1)
    kpos = jnp.concatenate([PAST_LEN - buf + jnp.arange(buf), qpos])
    diff = qpos[:, None] - kpos[None, :]
    mask = (diff >= 0) & (diff < WINDOW)
    o = sink_attend(q, kall, vall, mask, sink)
    y = o.reshape(DB, T, SWA_Q_HEADS * HEAD_DIM) @ w_o + b_o
    return y, kall[:, T:], vall[:, T:]


def fox_project(xn, w_qkv, w_f, b_f):
    lead = xn.shape[:-1]
    q, k, v = jnp.split(xn @ w_qkv, 3, axis=-1)
    q = q.reshape(*lead, FOX_HEADS, HEAD_DIM)
    k = k.reshape(*lead, FOX_HEADS, HEAD_DIM)
    v = v.reshape(*lead, FOX_HEADS, HEAD_DIM)
    logf = jax.nn.log_sigmoid((xn @ w_f + b_f).astype(jnp.float32))
    return q, k, v, logf


def fox_prompt(xn, w_qkv, w_f, b_f, w_o):
    B, S, _ = xn.shape
    q, k, v, logf = fox_project(xn, w_qkv, w_f, b_f)
    dcum = jnp.cumsum(logf, axis=1)
    nb = S // FOX_BLOCK
    qb = q.reshape(B, nb, FOX_BLOCK, FOX_HEADS, HEAD_DIM).swapaxes(0, 1)
    db = dcum.reshape(B, nb, FOX_BLOCK, FOX_HEADS).swapaxes(0, 1)
    qpos = jnp.arange(S).reshape(nb, FOX_BLOCK)
    kpos = jnp.arange(S)
    dk = dcum.transpose(0, 2, 1)
    scale = HEAD_DIM ** -0.5

    def block(args):
        qblk, dblk, qp = args
        s = jnp.einsum('bqhd,bshd->bhqs', qblk, k).astype(jnp.float32) * scale
        s = s + dblk.transpose(0, 2, 1)[..., None] - dk[:, :, None, :]
        s = jnp.where(qp[:, None] >= kpos[None, :], s, NEG)
        p = jax.nn.softmax(s, axis=-1).astype(v.dtype)
        return jnp.einsum('bhqs,bshd->bqhd', p, v)

    o = lax.map(block, (qb, db, qpos))
    o = o.swapaxes(0, 1).reshape(B, S, FOX_HEADS * HEAD_DIM)
    return o @ w_o, k, v, logf


def fox_sample(xn, cache_k, cache_v, cache_logf, page_table, w_qkv, w_f, b_f, w_o):
    DB, T, _ = xn.shape
    q, k, v, logf = fox_project(xn, w_qkv, w_f, b_f)
    P = page_table.shape[1] * PAGE_SIZE
    kp = cache_k[page_table].reshape(DB, P, FOX_HEADS, HEAD_DIM)
    vp = cache_v[page_table].reshape(DB, P, FOX_HEADS, HEAD_DIM)
    lp = cache_logf[page_table].reshape(DB, P, FOX_HEADS).astype(jnp.float32)
    dpast = jnp.cumsum(lp, axis=1)
    dnew = dpast[:, -1:] + jnp.cumsum(logf, axis=1)
    dq = dnew.transpose(0, 2, 1)[..., None]
    scale = HEAD_DIM ** -0.5
    s_past = jnp.einsum('bqhd,bshd->bhqs', q, kp).astype(jnp.float32) * scale - dpast.transpose(0, 2, 1)[:, :, None, :] + dq
    s_new = jnp.einsum('bqhd,bshd->bhqs', q, k).astype(jnp.float32) * scale - dnew.transpose(0, 2, 1)[:, :, None, :] + dq
    causal = jnp.arange(T)[:, None] >= jnp.arange(T)[None, :]
    s_new = jnp.where(causal, s_new, NEG)
    p = jax.nn.softmax(jnp.concatenate([s_past, s_new], axis=-1), axis=-1).astype(v.dtype)
    o = jnp.einsum('bhqs,bshd->bqhd', p[..., :P], vp) + jnp.einsum('bhqs,bshd->bqhd', p[..., P:], v)
    return o.reshape(DB, T, FOX_HEADS * HEAD_DIM) @ w_o, k, v, logf


def conv_mixer(xn, state, w_in, conv_w, w_out):
    gb, gc, h = jnp.split(xn @ w_in, 3, axis=-1)
    u = gc * h
    T = u.shape[1]
    ue = jnp.concatenate([state, u], axis=1)
    conv = conv_w[0] * ue[:, 0:T]
    for j in range(1, CONV_WIDTH):
        conv = conv + conv_w[j] * ue[:, j:j + T]
    y = (gb * conv) @ w_out
    return y, ue[:, T:]


def swiglu(h, g, w_gate, w_up, w_down):
    hn = rmsnorm(h, g)
    return (jax.nn.silu(hn @ w_gate) * (hn @ w_up)) @ w_down


def ple_add(h, p, g, w_gate, w_proj):
    gate = jax.nn.sigmoid(rmsnorm(h, g) @ w_gate)
    return h + gate * (p @ w_proj)


def setup_inputs(seed: int = 0) -> dict:
    key = jax.random.key(seed)
    ks = iter(jax.random.split(key, 48))

    def nrm(shape, scale=1.0):
        return jax.random.normal(next(ks), shape, jnp.float32) * scale

    n_pages = PAST_LEN // PAGE_SIZE
    n_pool = (5 * DEC_BATCH * n_pages) // 4
    swa_buf = min(WINDOW, PAST_LEN)
    qkv_w = (SWA_Q_HEADS + 2 * SWA_KV_HEADS) * HEAD_DIM
    fox_base = jnp.linspace(1.0, 6.0, FOX_HEADS, dtype=jnp.float32)
    page_table = jax.random.permutation(next(ks), n_pool)[:DEC_BATCH * n_pages].reshape(DEC_BATCH, n_pages).astype(jnp.int32)
    return {
        "x_prompt": nrm((BATCH, SEQ, D_MODEL)),
        "x_sample": nrm((DEC_BATCH, DEC_SEQ, D_MODEL)),
        "cache_swa_k": nrm((N_SWA, DEC_BATCH, swa_buf, SWA_KV_HEADS, HEAD_DIM)),
        "cache_swa_v": nrm((N_SWA, DEC_BATCH, swa_buf, SWA_KV_HEADS, HEAD_DIM)),
        "cache_fox_k": nrm((N_FOX, n_pool, PAGE_SIZE, FOX_HEADS, HEAD_DIM)),
        "cache_fox_v": nrm((N_FOX, n_pool, PAGE_SIZE, FOX_HEADS, HEAD_DIM)),
        "cache_fox_logf": jax.nn.log_sigmoid(fox_base + nrm((N_FOX, n_pool, PAGE_SIZE, FOX_HEADS), 0.5)),
        "state_conv": nrm((N_CONV, DEC_BATCH, CONV_WIDTH - 1, D_MODEL)),
        "page_table": page_table,
        "p_prompt": nrm((DEPTH, BATCH, SEQ, PLE_DIM)),
        "p_sample": nrm((DEPTH, DEC_BATCH, DEC_SEQ, PLE_DIM)),
        "norm_mix": 1.0 + nrm((DEPTH, D_MODEL), 0.05),
        "norm_ffn": 1.0 + nrm((DEPTH, D_MODEL), 0.05),
        "norm_ple": 1.0 + nrm((DEPTH, D_MODEL), 0.05),
        "norm_final": 1.0 + nrm((D_MODEL,), 0.05),
        "swa_w_qkv": nrm((N_SWA, D_MODEL, qkv_w), D_MODEL ** -0.5),
        "swa_b_qkv": nrm((N_SWA, qkv_w), 0.02),
        "swa_w_o": nrm((N_SWA, SWA_Q_HEADS * HEAD_DIM, D_MODEL), (SWA_Q_HEADS * HEAD_DIM) ** -0.5),
        "swa_b_o": nrm((N_SWA, D_MODEL), 0.02),
        "swa_sinks": nrm((N_SWA, SWA_Q_HEADS), 0.5),
        "fox_w_qkv": nrm((N_FOX, D_MODEL, 3 * FOX_HEADS * HEAD_DIM), D_MODEL ** -0.5),
        "fox_w_f": nrm((N_FOX, D_MODEL, FOX_HEADS), D_MODEL ** -0.5),
        "fox_b_f": fox_base + nrm((N_FOX, FOX_HEADS), 0.1),
        "fox_w_o": nrm((N_FOX, FOX_HEADS * HEAD_DIM, D_MODEL), (FOX_HEADS * HEAD_DIM) ** -0.5),
        "conv_w_in": nrm((N_CONV, D_MODEL, 3 * D_MODEL), D_MODEL ** -0.5),
        "conv_w": nrm((N_CONV, CONV_WIDTH, D_MODEL), CONV_WIDTH ** -0.5),
        "conv_w_out": nrm((N_CONV, D_MODEL, D_MODEL), D_MODEL ** -0.5),
        "ffn_w_gate": nrm((DEPTH, D_MODEL, FFN_DIM), D_MODEL ** -0.5),
        "ffn_w_up": nrm((DEPTH, D_MODEL, FFN_DIM), D_MODEL ** -0.5),
        "ffn_w_down": nrm((DEPTH, FFN_DIM, D_MODEL), FFN_DIM ** -0.5),
        "ple_w_gate": nrm((DEPTH, D_MODEL, D_MODEL), D_MODEL ** -0.5),
        "ple_w_proj": nrm((DEPTH, PLE_DIM, D_MODEL), PLE_DIM ** -0.5),
    }


def reference(x_prompt, x_sample, cache_swa_k, cache_swa_v, cache_fox_k, cache_fox_v, cache_fox_logf, state_conv, page_table, p_prompt, p_sample,
              norm_mix, norm_ffn, norm_ple, norm_final, swa_w_qkv, swa_b_qkv, swa_w_o, swa_b_o, swa_sinks,
              fox_w_qkv, fox_w_f, fox_b_f, fox_w_o, conv_w_in, conv_w, conv_w_out,
              ffn_w_gate, ffn_w_up, ffn_w_down, ple_w_gate, ple_w_proj):
    hp, hs = x_prompt, x_sample
    swa_kp, swa_vp, swa_ks, swa_vs = [], [], [], []
    fox_kp, fox_vp, fox_lp, fox_ks, fox_vs, fox_ls = [], [], [], [], [], []
    conv_p, conv_s = [], []
    for i in range(DEPTH):
        j = i // N_MIXERS
        kind = i % N_MIXERS
        xp = rmsnorm(hp, norm_mix[i])
        xs = rmsnorm(hs, norm_mix[i])
        if kind == 0:
            yp, kp, vp = swa_prompt(xp, swa_w_qkv[j], swa_b_qkv[j], swa_w_o[j], swa_b_o[j], swa_sinks[j])
            ys, ks_, vs_ = swa_sample(xs, cache_swa_k[j], cache_swa_v[j], swa_w_qkv[j], swa_b_qkv[j], swa_w_o[j], swa_b_o[j], swa_sinks[j])
            swa_kp.append(kp); swa_vp.append(vp); swa_ks.append(ks_); swa_vs.append(vs_)
        elif kind == 1:
            yp, kp, vp, lp = fox_prompt(xp, fox_w_qkv[j], fox_w_f[j], fox_b_f[j], fox_w_o[j])
            ys, ks_, vs_, ls_ = fox_sample(xs, cache_fox_k[j], cache_fox_v[j], cache_fox_logf[j], page_table, fox_w_qkv[j], fox_w_f[j], fox_b_f[j], fox_w_o[j])
            fox_kp.append(kp); fox_vp.append(vp); fox_lp.append(lp)
            fox_ks.append(ks_); fox_vs.append(vs_); fox_ls.append(ls_)
        else:
            zero_state = jnp.zeros((xp.shape[0], CONV_WIDTH - 1, D_MODEL), xp.dtype)
            yp, sp = conv_mixer(xp, zero_state, conv_w_in[j], conv_w[j], conv_w_out[j])
            ys, ss = conv_mixer(xs, state_conv[j], conv_w_in[j], conv_w[j], conv_w_out[j])
            conv_p.append(sp); conv_s.append(ss)
        hp = hp + yp
        hs = hs + ys
        hp = hp + swiglu(hp, norm_ffn[i], ffn_w_gate[i], ffn_w_up[i], ffn_w_down[i])
        hs = hs + swiglu(hs, norm_ffn[i], ffn_w_gate[i], ffn_w_up[i], ffn_w_down[i])
        hp = ple_add(hp, p_prompt[i], norm_ple[i], ple_w_gate[i], ple_w_proj[i])
        hs = ple_add(hs, p_sample[i], norm_ple[i], ple_w_gate[i], ple_w_proj[i])
    y_prompt = rmsnorm(hp, norm_final)
    y_sample = rmsnorm(hs, norm_final)
    return (y_prompt, y_sample,
            jnp.stack(swa_kp), jnp.stack(swa_vp), jnp.stack(swa_ks), jnp.stack(swa_vs),
            jnp.stack(fox_kp), jnp.stack(fox_vp), jnp.stack(fox_lp),
            jnp.stack(fox_ks), jnp.stack(fox_vs), jnp.stack(fox_ls),
            jnp.stack(conv_p), jnp.stack(conv_s))
```

```python
import functools

import numpy as np
import jax
import jax.numpy as jnp
from jax import lax
from jax.experimental import pallas as pl
from jax.experimental.pallas import tpu as pltpu

F32 = jnp.float32
BF16 = jnp.bfloat16

HEAD_DIM = 64
HALF = HEAD_DIM // 2
SWA_Q_HEADS = 16
SWA_KV_HEADS = 4
SWA_GROUP = SWA_Q_HEADS // SWA_KV_HEADS
WINDOW = 128
FOX_HEADS = 16
PAGE_SIZE = 128
PAST_LEN = 8192
N_MIXERS = 3
CONV_WIDTH = 3
ROPE_THETA = 10000.0
EPS = 1e-6
NEG = -1e30
SCALE = HEAD_DIM ** -0.5

LANES = 128
SUBLANES = 8
BF16_ROWS = 16
VMEM_LIMIT = 56 * 1024 * 1024
AUG_COL = HEAD_DIM
FOX_PAGES_PER_STEP = 8
LOGF_PAGES_PER_STEP = 16


def _params(*sem):
    return pltpu.CompilerParams(dimension_semantics=sem, vmem_limit_bytes=VMEM_LIMIT)


def _const_spec(shape):
    nd = len(shape)
    return pl.BlockSpec(shape, lambda *_: (0,) * nd, pipeline_mode=pl.Buffered(1))


def _rms(x, g):
    ms = jnp.mean(x * x, axis=-1, keepdims=True)
    return x * lax.rsqrt(ms + EPS) * g


def _dot(a, b):
    return jnp.dot(a, b, preferred_element_type=F32)


def _dot_nt(a, b):
    return lax.dot_general(a, b, (((1,), (1,)), ((), ())), preferred_element_type=F32)


def _dot_exact(a, b):
    return jnp.dot(a, b, preferred_element_type=F32, precision=lax.Precision.HIGHEST)


def _sigmoid(x):
    return 1.0 / (1.0 + jnp.exp(-x))


def _log_sigmoid(z):
    return jnp.minimum(z, 0.0) - jnp.log1p(jnp.exp(-jnp.abs(z)))


def _swa_proj_kernel(x_ref, g_ref, w_ref, b_ref, cos_ref, sin_ref, q_ref, k_ref, v_ref):
    xn = _rms(x_ref[...], g_ref[...]).astype(BF16)
    qkv = _dot(xn, w_ref[...]) + b_ref[...]
    nq, nk = q_ref.shape[1], k_ref.shape[1]
    cos, sin = cos_ref[...], sin_ref[...]
    lane = lax.broadcasted_iota(jnp.int32, cos.shape, 1)
    first_half = (lane % HEAD_DIM) < HALF

    def rope(xs):
        sw = jnp.where(first_half, pltpu.roll(xs, LANES - HALF, 1), pltpu.roll(xs, HALF, 1))
        return xs * cos + sw * sin

    for j in range(nq // LANES):
        q_ref[:, j * LANES:(j + 1) * LANES] = rope(qkv[:, j * LANES:(j + 1) * LANES]).astype(q_ref.dtype)
    for j in range(nk // LANES):
        k_ref[:, j * LANES:(j + 1) * LANES] = rope(qkv[:, nq + j * LANES:nq + (j + 1) * LANES])
    v_ref[...] = qkv[:, nq + nk:]


def _swa_proj(x, g, w, b, cos_t, sin_t, *, tm, q_dtype):
    t, d = x.shape
    nq = SWA_Q_HEADS * HEAD_DIM
    nk = SWA_KV_HEADS * HEAD_DIM
    n_pos_blocks = cos_t.shape[0] // tm
    return pl.pallas_call(
        _swa_proj_kernel,
        grid=(t // tm,),
        in_specs=[
            pl.BlockSpec((tm, d), lambda i: (i, 0)),
            _const_spec((1, d)),
            _const_spec(w.shape),
            _const_spec((1, w.shape[1])),
            pl.BlockSpec((tm, LANES), lambda i: (i % n_pos_blocks, 0)),
            pl.BlockSpec((tm, LANES), lambda i: (i % n_pos_blocks, 0)),
        ],
        out_specs=[
            pl.BlockSpec((tm, nq), lambda i: (i, 0)),
            pl.BlockSpec((tm, nk), lambda i: (i, 0)),
            pl.BlockSpec((tm, nk), lambda i: (i, 0)),
        ],
        out_shape=[
            jax.ShapeDtypeStruct((t, nq), q_dtype),
            jax.ShapeDtypeStruct((t, nk), F32),
            jax.ShapeDtypeStruct((t, nk), F32),
        ],
        compiler_params=_params("parallel"),
        name="swa_proj",
    )(x, g, w, b, cos_t, sin_t)


def _swa_prompt_kernel(sink_ref, q_ref, kp_ref, kc_ref, vp_ref, vc_ref, o_ref):
    i = pl.program_id(1)
    w = WINDOW
    kband = jnp.concatenate([kp_ref[...], kc_ref[...]], axis=0)
    vband = jnp.concatenate([vp_ref[...], vc_ref[...]], axis=0)
    qi = lax.broadcasted_iota(jnp.int32, (w, 2 * w), 0)
    kj = lax.broadcasted_iota(jnp.int32, (w, 2 * w), 1)
    dist = qi + w - kj
    mask = (dist >= 0) & (dist < w) & ((kj >= w) | (i > 0))
    lane = lax.broadcasted_iota(jnp.int32, (2 * w, LANES), 1)
    low = lane < HEAD_DIM
    for kvp in range(SWA_KV_HEADS // 2):
        kt = kband[:, kvp * LANES:(kvp + 1) * LANES]
        vt = vband[:, kvp * LANES:(kvp + 1) * LANES]
        kz = [jnp.where(low, kt, 0.0).astype(BF16), jnp.where(low, 0.0, kt).astype(BF16)]
        vz = [jnp.where(low, vt, 0.0).astype(BF16), jnp.where(low, 0.0, vt).astype(BF16)]
        for g in range(SWA_GROUP):
            c0 = g * SWA_KV_HEADS * HEAD_DIM + kvp * LANES
            qt = q_ref[:, c0:c0 + LANES]
            o_pair = None
            for e in range(2):
                kv = kvp * 2 + e
                s = _dot_nt(qt, kz[e]) * SCALE
                s = jnp.where(mask, s, NEG)
                sk = sink_ref[kv * SWA_GROUP + g]
                m = jnp.maximum(jnp.max(s, axis=-1, keepdims=True), sk)
                ex = jnp.exp(s - m)
                den = jnp.sum(ex, axis=-1, keepdims=True) + jnp.exp(sk - m)
                p = (ex / den).astype(BF16)
                pv = _dot(p, vz[e])
                o_pair = pv if o_pair is None else o_pair + pv
            o_ref[:, c0:c0 + LANES] = o_pair.astype(o_ref.dtype)


def _swa_prompt_attn(q, k, v, sinks, *, batch, seq):
    nb = seq // WINDOW
    nq, nk = q.shape[1], k.shape[1]
    cur = lambda b, i: (b * nb + i, 0)
    prev = lambda b, i: (b * nb + jnp.maximum(i - 1, 0), 0)
    return pl.pallas_call(
        _swa_prompt_kernel,
        grid=(batch, nb),
        in_specs=[
            pl.BlockSpec(memory_space=pltpu.SMEM),
            pl.BlockSpec((WINDOW, nq), cur),
            pl.BlockSpec((WINDOW, nk), prev),
            pl.BlockSpec((WINDOW, nk), cur),
            pl.BlockSpec((WINDOW, nk), prev),
            pl.BlockSpec((WINDOW, nk), cur),
        ],
        out_specs=pl.BlockSpec((WINDOW, nq), cur),
        out_shape=jax.ShapeDtypeStruct(q.shape, BF16),
        compiler_params=_params("parallel", "parallel"),
        name="swa_prompt_attn",
    )(sinks, q, k, k, v, v)


def _swa_sample_kernel(sink_ref, q_ref, kn_ref, vn_ref, ck_ref, cv_ref, o_ref, *, n_new):
    bb = q_ref.shape[0]
    buf = ck_ref.shape[1]
    rows = SWA_Q_HEADS * SUBLANES
    kvw = SWA_KV_HEADS * HEAD_DIM
    lane = lax.broadcasted_iota(jnp.int32, (SUBLANES, kvw), 1)
    head_mask = [(lane // HEAD_DIM) == kv for kv in range(SWA_KV_HEADS)]
    t = lax.broadcasted_iota(jnp.int32, (rows, 1), 0) % SUBLANES
    valid_c = lax.broadcasted_iota(jnp.int32, (rows, buf), 1) > t
    tn = lax.broadcasted_iota(jnp.int32, (rows, BF16_ROWS), 1)
    valid_n = (tn <= t) & (tn < n_new)
    sk = sink_ref[...]

    def body(b, carry):
        q8 = q_ref[b]
        pieces = []
        for kv in range(SWA_KV_HEADS):
            for g in range(SWA_GROUP):
                pieces.append(jnp.where(head_mask[kv], q8[:, g * kvw:(g + 1) * kvw], 0.0))
        qbd = jnp.concatenate(pieces, axis=0).astype(BF16)
        s_c = jnp.where(valid_c, _dot_nt(qbd, ck_ref[b].astype(BF16)) * SCALE, NEG)
        s_n = jnp.where(valid_n, _dot_nt(qbd, kn_ref[b].astype(BF16)) * SCALE, NEG)
        m = jnp.maximum(jnp.maximum(jnp.max(s_c, axis=-1, keepdims=True),
                                    jnp.max(s_n, axis=-1, keepdims=True)), sk)
        e_c = jnp.exp(s_c - m)
        e_n = jnp.exp(s_n - m)
        den = jnp.sum(e_c, axis=-1, keepdims=True) + jnp.sum(e_n, axis=-1, keepdims=True) + jnp.exp(sk - m)
        o = (_dot((e_c / den).astype(BF16), cv_ref[b].astype(BF16))
             + _dot((e_n / den).astype(BF16), vn_ref[b].astype(BF16)))
        for g in range(SWA_GROUP):
            acc = None
            for kv in range(SWA_KV_HEADS):
                r0 = (kv * SWA_GROUP + g) * SUBLANES
                part = jnp.where(head_mask[kv], o[r0:r0 + SUBLANES], 0.0)
                acc = part if acc is None else acc + part
            o_ref[b, :, g * kvw:(g + 1) * kvw] = acc
        return carry

    lax.fori_loop(0, bb, body, 0)


def _swa_sample_attn(q8, kn16, vn16, cache_k, cache_v, sink_col, *, n_new, bb):
    db, buf, kvw = cache_k.shape
    nq = q8.shape[2]
    return pl.pallas_call(
        functools.partial(_swa_sample_kernel, n_new=n_new),
        grid=(db // bb,),
        in_specs=[
            _const_spec(sink_col.shape),
            pl.BlockSpec((bb, SUBLANES, nq), lambda i: (i, 0, 0)),
            pl.BlockSpec((bb, BF16_ROWS, kvw), lambda i: (i, 0, 0)),
            pl.BlockSpec((bb, BF16_ROWS, kvw), lambda i: (i, 0, 0)),
            pl.BlockSpec((bb, buf, kvw), lambda i: (i, 0, 0)),
            pl.BlockSpec((bb, buf, kvw), lambda i: (i, 0, 0)),
        ],
        out_specs=pl.BlockSpec((bb, SUBLANES, nq), lambda i: (i, 0, 0)),
        out_shape=jax.ShapeDtypeStruct(q8.shape, F32),
        compiler_params=_params("parallel"),
        name="swa_sample_attn",
    )(sink_col, q8, kn16, vn16, cache_k, cache_v)


def _fox_logf(xn, wf_ref, bf_ref):
    return _log_sigmoid(_dot(xn, wf_ref[...]) + bf_ref[...])


def _fox_proj_prompt_kernel(x_ref, g_ref, w_ref, wf_ref, bf_ref, pq_ref, pk_ref, oq_ref, ok_ref,
                            k_ref, v_ref, lf_ref, qa_ref, ka_ref, vb_ref, carry_ref, *, tiles_per_seq):
    i = pl.program_id(0)
    tm = x_ref.shape[0]
    d = k_ref.shape[1]

    @pl.when(i % tiles_per_seq == 0)
    def _():
        carry_ref[...] = jnp.zeros_like(carry_ref)

    xn = _rms(x_ref[...], g_ref[...]).astype(BF16)
    qkv = _dot(xn, w_ref[...])
    k_ref[...] = qkv[:, d:2 * d]
    v_ref[...] = qkv[:, 2 * d:]
    vb_ref[...] = qkv[:, 2 * d:].astype(BF16)
    logf = _fox_logf(xn, wf_ref, bf_ref)
    lf_ref[...] = logf[:, :FOX_HEADS]

    r = lax.broadcasted_iota(jnp.int32, (tm, tm), 0)
    c = lax.broadcasted_iota(jnp.int32, (tm, tm), 1)
    ltri = jnp.where(c <= r, 1.0, 0.0).astype(F32)
    dcum = _dot_exact(ltri, logf) + carry_ref[...]
    carry_ref[...] = dcum[tm - 1:tm, :]

    hi = dcum.astype(BF16).astype(F32)
    r1 = dcum - hi
    mid = r1.astype(BF16).astype(F32)
    lo = r1 - mid
    grp = (lax.broadcasted_iota(jnp.int32, (tm, LANES), 1) // FOX_HEADS) % 3
    split = jnp.where(grp == 0, hi, jnp.where(grp == 1, mid, lo)).astype(BF16)
    aug_q = _dot(split, pq_ref[...]) + oq_ref[...]
    aug_k = _dot(split, pk_ref[...]) + ok_ref[...]

    low = lax.broadcasted_iota(jnp.int32, (tm, LANES), 1) < HEAD_DIM
    for h in range(FOX_HEADS):
        p, e = h // 2, h % 2
        qp = qkv[:, p * LANES:(p + 1) * LANES] * SCALE
        kp = qkv[:, d + p * LANES:d + (p + 1) * LANES]
        if e == 1:
            qp = pltpu.roll(qp, HEAD_DIM, 1)
            kp = pltpu.roll(kp, HEAD_DIM, 1)
        qa_ref[:, h * LANES:(h + 1) * LANES] = jnp.where(low, qp, aug_q[:, h * LANES:(h + 1) * LANES]).astype(BF16)
        ka_ref[:, h * LANES:(h + 1) * LANES] = jnp.where(low, kp, aug_k[:, h * LANES:(h + 1) * LANES]).astype(BF16)


def _fox_aug_constants():
    pq = np.zeros((LANES, FOX_HEADS * LANES), np.float32)
    pk = np.zeros((LANES, FOX_HEADS * LANES), np.float32)
    oq = np.zeros((1, FOX_HEADS * LANES), np.float32)
    ok = np.zeros((1, FOX_HEADS * LANES), np.float32)
    for h in range(FOX_HEADS):
        base = h * LANES + AUG_COL
        for c in range(3):
            pk[c * FOX_HEADS + h, base + c] = -1.0
            oq[0, base + c] = 1.0
            pq[c * FOX_HEADS + h, base + 3 + c] = 1.0
            ok[0, base + 3 + c] = 1.0
    return jnp.asarray(pq, BF16), jnp.asarray(pk, BF16), jnp.asarray(oq), jnp.asarray(ok)


def _fox_proj_prompt(x, g, w, wf, bf, *, tm, seq):
    t, d = x.shape
    pq, pk, oq, ok = _fox_aug_constants()
    aw = FOX_HEADS * LANES
    row = lambda i: (i, 0)
    return pl.pallas_call(
        functools.partial(_fox_proj_prompt_kernel, tiles_per_seq=seq // tm),
        grid=(t // tm,),
        in_specs=[
            pl.BlockSpec((tm, d), row),
            _const_spec((1, d)),
            _const_spec(w.shape),
            _const_spec(wf.shape),
            _const_spec(bf.shape),
            _const_spec(pq.shape),
            _const_spec(pk.shape),
            _const_spec(oq.shape),
            _const_spec(ok.shape),
        ],
        out_specs=[
            pl.BlockSpec((tm, d), row),
            pl.BlockSpec((tm, d), row),
            pl.BlockSpec((tm, FOX_HEADS), row),
            pl.BlockSpec((tm, aw), row),
            pl.BlockSpec((tm, aw), row),
            pl.BlockSpec((tm, d), row),
        ],
        out_shape=[
            jax.ShapeDtypeStruct((t, d), F32),
            jax.ShapeDtypeStruct((t, d), F32),
            jax.ShapeDtypeStruct((t, FOX_HEADS), F32),
            jax.ShapeDtypeStruct((t, aw), BF16),
            jax.ShapeDtypeStruct((t, aw), BF16),
            jax.ShapeDtypeStruct((t, d), BF16),
        ],
        scratch_shapes=[pltpu.VMEM((1, LANES), F32)],
        compiler_params=_params("arbitrary"),
        name="fox_proj_prompt",
    )(x, g, w, wf, bf, pq, pk, oq, ok)


def _fox_proj_sample_kernel(x_ref, g_ref, w_ref, wf_ref, bf_ref, dlast_ref,
                            q_ref, k_ref, v_ref, lf_ref, dn_ref, *, n_new):
    tm = x_ref.shape[0]
    d = k_ref.shape[1]
    xn = _rms(x_ref[...], g_ref[...]).astype(BF16)
    qkv = _dot(xn, w_ref[...])
    q_ref[...] = qkv[:, :d]
    k_ref[...] = qkv[:, d:2 * d]
    v_ref[...] = qkv[:, 2 * d:]
    logf = _fox_logf(xn, wf_ref, bf_ref)[:, :FOX_HEADS]
    lf_ref[...] = logf
    r = lax.broadcasted_iota(jnp.int32, (tm, tm), 0)
    c = lax.broadcasted_iota(jnp.int32, (tm, tm), 1)
    seg = jnp.where((c <= r) & (c // n_new == r // n_new), 1.0, 0.0).astype(F32)
    dn_ref[...] = dlast_ref[...] + _dot_exact(seg, logf)


def _fox_proj_sample(x, g, w, wf, bf, dlast_rows, *, n_new):
    t, d = x.shape
    return pl.pallas_call(
        functools.partial(_fox_proj_sample_kernel, n_new=n_new),
        grid=(1,),
        in_specs=[
            _const_spec((t, d)),
            _const_spec((1, d)),
            _const_spec(w.shape),
            _const_spec(wf.shape),
            _const_spec(bf.shape),
            _const_spec(dlast_rows.shape),
        ],
        out_specs=[
            _const_spec((t, d)), _const_spec((t, d)), _const_spec((t, d)),
            _const_spec((t, FOX_HEADS)), _const_spec((t, FOX_HEADS)),
        ],
        out_shape=[
            jax.ShapeDtypeStruct((t, d), F32),
            jax.ShapeDtypeStruct((t, d), F32),
            jax.ShapeDtypeStruct((t, d), F32),
            jax.ShapeDtypeStruct((t, FOX_HEADS), F32),
            jax.ShapeDtypeStruct((t, FOX_HEADS), F32),
        ],
        compiler_params=_params("arbitrary"),
        name="fox_proj_sample",
    )(x, g, w, wf, bf, dlast_rows)


def _fox_flash_kernel(qi_ref, kj_ref, q_ref, k_ref, v_ref, o_ref, m_ref, l_ref, acc_ref):
    step = pl.program_id(2)
    i, j = qi_ref[step], kj_ref[step]
    tq, tk = q_ref.shape[0], k_ref.shape[0]

    @pl.when(j == 0)
    def _():
        m_ref[...] = jnp.full_like(m_ref, NEG)
        l_ref[...] = jnp.zeros_like(l_ref)
        acc_ref[...] = jnp.zeros_like(acc_ref)

    ahead = (lax.broadcasted_iota(jnp.int32, (tq, tk), 1)
             - lax.broadcasted_iota(jnp.int32, (tq, tk), 0))
    limit = jnp.where(i == j, 0, tk)
    low_k = lax.broadcasted_iota(jnp.int32, (tk, LANES), 1) < HEAD_DIM
    low_q = lax.broadcasted_iota(jnp.int32, (tq, LANES), 1) < HEAD_DIM
    v = v_ref[...]
    keep = [jnp.where(low_k, 1.0, 0.0).astype(BF16), jnp.where(low_k, 0.0, 1.0).astype(BF16)]
    pvs, alphas = [], []
    for e in range(2):
        s = _dot_nt(q_ref[:, e * LANES:(e + 1) * LANES], k_ref[:, e * LANES:(e + 1) * LANES])
        s = jnp.where(ahead > limit, NEG, s)
        m_prev = m_ref[e]
        m_new = jnp.maximum(m_prev, jnp.max(s, axis=-1, keepdims=True))
        alpha = jnp.exp(m_prev - m_new)
        p = jnp.exp(s - m_new)
        l_ref[e] = alpha * l_ref[e] + jnp.sum(p, axis=-1, keepdims=True)
        m_ref[e] = m_new
        pvs.append(_dot(p.astype(BF16), v * keep[e]))
        alphas.append(alpha)
    acc_ref[...] = acc_ref[...] * jnp.where(low_q, alphas[0], alphas[1]) + pvs[0] + pvs[1]

    @pl.when(j == i)
    def _():
        o_ref[...] = (acc_ref[...] / jnp.where(low_q, l_ref[0], l_ref[1])).astype(o_ref.dtype)


def _fox_flash(qa, ka, vb, *, batch, seq, tq):
    nq = seq // tq
    pairs = [(i, j) for i in range(nq) for j in range(i + 1)]
    qi = jnp.asarray([p[0] for p in pairs], jnp.int32)
    kj = jnp.asarray([p[1] for p in pairs], jnp.int32)
    t, d = vb.shape
    grid_spec = pltpu.PrefetchScalarGridSpec(
        num_scalar_prefetch=2,
        grid=(batch, FOX_HEADS // 2, len(pairs)),
        in_specs=[
            pl.BlockSpec((tq, 2 * LANES), lambda b, p, s, qi, kj: (b * nq + qi[s], p)),
            pl.BlockSpec((tq, 2 * LANES), lambda b, p, s, qi, kj: (b * nq + kj[s], p)),
            pl.BlockSpec((tq, LANES), lambda b, p, s, qi, kj: (b * nq + kj[s], p)),
        ],
        out_specs=pl.BlockSpec((tq, LANES), lambda b, p, s, qi, kj: (b * nq + qi[s], p)),
        scratch_shapes=[
            pltpu.VMEM((2, tq, 1), F32),
            pltpu.VMEM((2, tq, 1), F32),
            pltpu.VMEM((tq, LANES), F32),
        ],
    )
    return pl.pallas_call(
        _fox_flash_kernel,
        grid_spec=grid_spec,
        out_shape=jax.ShapeDtypeStruct((t, d), BF16),
        compiler_params=_params("parallel", "parallel", "arbitrary"),
        name="fox_flash",
    )(qi, kj, qa, ka, vb)


def _fox_dpast_kernel(pt_ref, *refs):
    n = LOGF_PAGES_PER_STEP
    page_refs, o_ref, carry_ref = refs[:n], refs[n], refs[n + 1]
    j = pl.program_id(1)

    @pl.when(j == 0)
    def _():
        carry_ref[...] = jnp.zeros_like(carry_ref)

    x = jnp.concatenate([r[0] for r in page_refs], axis=0)
    rows = x.shape[0]
    li = lax.broadcasted_iota(jnp.int32, (LANES, LANES), 0)
    lj = lax.broadcasted_iota(jnp.int32, (LANES, LANES), 1)
    same_head = (li % FOX_HEADS) == (lj % FOX_HEADS)
    within = jnp.where(same_head & (li <= lj), 1.0, 0.0).astype(F32)
    total = jnp.where(same_head, 1.0, 0.0).astype(F32)
    ri = lax.broadcasted_iota(jnp.int32, (rows, rows), 0)
    rj = lax.broadcasted_iota(jnp.int32, (rows, rows), 1)
    above = jnp.where(rj < ri, 1.0, 0.0).astype(F32)
    tot = _dot_exact(x, total)
    d = _dot_exact(x, within) + _dot_exact(above, tot) + carry_ref[...]
    o_ref[0] = d
    carry_ref[...] = carry_ref[...] + jnp.sum(tot, axis=0, keepdims=True)


def _fox_dpast(logf_pages, page_table):
    db, n_pages = page_table.shape
    n = LOGF_PAGES_PER_STEP
    rows = PAGE_SIZE * FOX_HEADS // LANES
    steps = n_pages // n

    def page_spec(k):
        return pl.BlockSpec((1, rows, LANES), lambda b, j, pt: (pt[b, j * n + k], 0, 0))

    grid_spec = pltpu.PrefetchScalarGridSpec(
        num_scalar_prefetch=1,
        grid=(db, steps),
        in_specs=[page_spec(k) for k in range(n)],
        out_specs=pl.BlockSpec((1, n * rows, LANES), lambda b, j, pt: (b, j, 0)),
        scratch_shapes=[pltpu.VMEM((1, LANES), F32)],
    )
    return pl.pallas_call(
        _fox_dpast_kernel,
        grid_spec=grid_spec,
        out_shape=jax.ShapeDtypeStruct((db, n_pages * rows, LANES), F32),
        compiler_params=_params("parallel", "arbitrary"),
        name="fox_dpast",
    )(page_table, *([logf_pages] * n))


def _fox_sample_kernel(pt_ref, *refs, n_new):
    n = FOX_PAGES_PER_STEP
    k_refs, v_refs = refs[:n], refs[n:2 * n]
    (dt_ref, q_ref, kn_ref, vn_ref, dq_ref, bn_ref, o_ref, qbd_ref, m_ref, l_ref, acc_ref) = refs[2 * n:]
    j = pl.program_id(1)
    rows = n_new * FOX_HEADS
    d = q_ref.shape[2]
    lane_head = lax.broadcasted_iota(jnp.int32, (FOX_HEADS, d), 1) // HEAD_DIM
    own = lane_head == lax.broadcasted_iota(jnp.int32, (FOX_HEADS, d), 0)

    @pl.when(j == 0)
    def _():
        q = q_ref[0]
        for t in range(n_new):
            qbd_ref[t * FOX_HEADS:(t + 1) * FOX_HEADS, :] = jnp.where(
                own, jnp.broadcast_to(q[t:t + 1, :] * SCALE, (FOX_HEADS, d)), 0.0).astype(BF16)
        m_ref[...] = jnp.full_like(m_ref, NEG)
        l_ref[...] = jnp.zeros_like(l_ref)
        acc_ref[...] = jnp.zeros_like(acc_ref)

    qbd = qbd_ref[...]
    dq = dq_ref[0]
    dt = dt_ref[0]
    bias = jnp.concatenate([dt] * n_new, axis=0)
    kc = jnp.concatenate([r[0] for r in k_refs], axis=0).astype(BF16)
    vc = jnp.concatenate([r[0] for r in v_refs], axis=0).astype(BF16)
    s = _dot_nt(qbd, kc) - bias + dq
    m_prev = m_ref[...]
    m_new = jnp.maximum(m_prev, jnp.max(s, axis=-1, keepdims=True))
    alpha = jnp.exp(m_prev - m_new)
    p = jnp.exp(s - m_new)
    l_ref[...] = alpha * l_ref[...] + jnp.sum(p, axis=-1, keepdims=True)
    m_ref[...] = m_new
    acc_ref[...] = acc_ref[...] * alpha + _dot(p.astype(BF16), vc)

    @pl.when(j == pl.num_programs(1) - 1)
    def _():
        s_n = _dot_nt(qbd, kn_ref[0].astype(BF16)) - bn_ref[0] + dq
        t_q = lax.broadcasted_iota(jnp.int32, s_n.shape, 0) // FOX_HEADS
        t_k = lax.broadcasted_iota(jnp.int32, s_n.shape, 1)
        s_n = jnp.where((t_k <= t_q) & (t_k < n_new), s_n, NEG)
        m_prev = m_ref[...]
        m_new = jnp.maximum(m_prev, jnp.max(s_n, axis=-1, keepdims=True))
        alpha = jnp.exp(m_prev - m_new)
        p_n = jnp.exp(s_n - m_new)
        l_fin = alpha * l_ref[...] + jnp.sum(p_n, axis=-1, keepdims=True)
        o = (acc_ref[...] * alpha + _dot(p_n.astype(BF16), vn_ref[0].astype(BF16))) / l_fin
        for t in range(n_new):
            blk = jnp.where(own, o[t * FOX_HEADS:(t + 1) * FOX_HEADS, :], 0.0)
            o_ref[0, t:t + 1, :] = jnp.sum(blk, axis=0, keepdims=True)


def _fox_sample_attn(k_pages, v_pages, page_table, dt, q, kn16, vn16, dq_col, bias_new, *, n_new):
    db, n_pages = page_table.shape
    n = FOX_PAGES_PER_STEP
    steps = n_pages // n
    d = q.shape[2]
    rows = n_new * FOX_HEADS
    keys = n * PAGE_SIZE

    def page_spec(k):
        return pl.BlockSpec((1, PAGE_SIZE, d), lambda b, j, pt: (pt[b, j * n + k], 0, 0))

    per_seq = lambda shape: pl.BlockSpec((1,) + shape, lambda b, j, pt: (b, 0, 0))
    grid_spec = pltpu.PrefetchScalarGridSpec(
        num_scalar_prefetch=1,
        grid=(db, steps),
        in_specs=[page_spec(k) for k in range(n)] + [page_spec(k) for k in range(n)] + [
            pl.BlockSpec((1, FOX_HEADS, keys), lambda b, j, pt: (b, 0, j)),
            per_seq((n_new, d)),
            per_seq((BF16_ROWS, d)),
            per_seq((BF16_ROWS, d)),
            per_seq((rows, 1)),
            per_seq((rows, BF16_ROWS)),
        ],
        out_specs=per_seq((n_new, d)),
        scratch_shapes=[
            pltpu.VMEM((rows, d), BF16),
            pltpu.VMEM((rows, 1), F32),
            pltpu.VMEM((rows, 1), F32),
            pltpu.VMEM((rows, d), F32),
        ],
    )
    return pl.pallas_call(
        functools.partial(_fox_sample_kernel, n_new=n_new),
        grid_spec=grid_spec,
        out_shape=jax.ShapeDtypeStruct((db, n_new, d), F32),
        compiler_params=_params("parallel", "arbitrary"),
        name="fox_sample_attn",
    )(page_table, *([k_pages] * n), *([v_pages] * n), dt, q, kn16, vn16, dq_col, bias_new)


def _conv_proj_kernel(*refs, seq, has_state):
    if has_state:
        x_ref, g_ref, w_ref, cw_ref, s1_ref, s2_ref, y_ref, u_ref, carry_ref = refs
    else:
        x_ref, g_ref, w_ref, cw_ref, y_ref, u_ref, carry_ref = refs
    i = pl.program_id(0)
    tm, d = x_ref.shape

    @pl.when(i == 0)
    def _():
        carry_ref[...] = jnp.zeros_like(carry_ref)

    xn = _rms(x_ref[...], g_ref[...]).astype(BF16)
    proj = _dot(xn, w_ref[...])
    gb, gc, hh = proj[:, :d], proj[:, d:2 * d], proj[:, 2 * d:]
    u = gc * hh
    row = lax.broadcasted_iota(jnp.int32, (tm, 1), 0)
    pos = (i * tm + row) % seq
    prev_tile = carry_ref[...]
    um1 = jnp.where(row == 0, prev_tile[7:8, :], pltpu.roll(u, 1, 0))
    um2 = jnp.where(row == 0, prev_tile[6:7, :], jnp.where(row == 1, prev_tile[7:8, :], pltpu.roll(u, 2, 0)))
    um1 = jnp.where(pos >= 1, um1, 0.0)
    um2 = jnp.where(pos >= 2, um2, 0.0)
    if has_state:
        um1 = um1 + s1_ref[...]
        um2 = um2 + s2_ref[...]
    cw = cw_ref[...]
    conv = cw[0:1, :] * um2 + cw[1:2, :] * um1 + cw[2:3, :] * u
    y_ref[...] = (gb * conv).astype(y_ref.dtype)
    carry_ref[...] = u[tm - SUBLANES:, :]
    if has_state:
        u_ref[...] = u
    else:
        u_ref[0] = u[tm - SUBLANES:, :]


def _conv_proj(x, g, w, cw, state_rows, *, tm, seq):
    t, d = x.shape
    has_state = state_rows is not None
    row = lambda i: (i, 0)
    in_specs = [pl.BlockSpec((tm, d), row), _const_spec((1, d)), _const_spec(w.shape), _const_spec(cw.shape)]
    args = [x, g, w, cw]
    if has_state:
        in_specs += [pl.BlockSpec((tm, d), row), pl.BlockSpec((tm, d), row)]
        args += list(state_rows)
        u_spec = pl.BlockSpec((tm, d), row)
        u_shape = jax.ShapeDtypeStruct((t, d), F32)
    else:
        tiles_per_seq = seq // tm
        u_spec = pl.BlockSpec((1, SUBLANES, d), lambda i: (i // tiles_per_seq, 0, 0))
        u_shape = jax.ShapeDtypeStruct((t // seq, SUBLANES, d), F32)
    return pl.pallas_call(
        functools.partial(_conv_proj_kernel, seq=seq, has_state=has_state),
        grid=(t // tm,),
        in_specs=in_specs,
        out_specs=[pl.BlockSpec((tm, d), row), u_spec],
        out_shape=[jax.ShapeDtypeStruct((t, d), BF16), u_shape],
        scratch_shapes=[pltpu.VMEM((SUBLANES, d), F32)],
        compiler_params=_params("arbitrary"),
        name="conv_proj",
    )(*args)


def _post_kernel(*refs, has_bias, final_norm, ffn_chunk):
    refs = list(refs)
    h_ref, o_ref, wo_ref = refs[:3]
    refs = refs[3:]
    bo_ref = refs.pop(0) if has_bias else None
    gf_ref, wg_ref, wu_ref, wd_ref, gp_ref, wpg_ref, p_ref, wpp_ref = refs[:8]
    refs = refs[8:]
    gn_ref = refs.pop(0) if final_norm else None
    out_ref, act_ref = refs

    y = _dot(o_ref[...].astype(BF16), wo_ref[...])
    if has_bias:
        y = y + bo_ref[...]
    h1 = h_ref[...] + y
    hn = _rms(h1, gf_ref[...]).astype(BF16)
    ffn = wg_ref.shape[1]
    for c in range(ffn // ffn_chunk):
        sl = slice(c * ffn_chunk, (c + 1) * ffn_chunk)
        gate = _dot(hn, wg_ref[:, sl])
        up = _dot(hn, wu_ref[:, sl])
        act_ref[:, sl] = (gate * _sigmoid(gate) * up).astype(BF16)
    h2 = h1 + _dot(act_ref[...], wd_ref[...])
    pg = _sigmoid(_dot(_rms(h2, gp_ref[...]).astype(BF16), wpg_ref[...]))
    h3 = h2 + pg * _dot(p_ref[...].astype(BF16), wpp_ref[...])
    if final_norm:
        h3 = _rms(h3, gn_ref[...])
    out_ref[...] = h3


def _post(h, o, wo, bo, gf, wg, wu, wd, gp, wpg, p, wpp, gn, *, tm, ffn_chunk):
    t, d = h.shape
    row = lambda i: (i, 0)
    in_specs = [pl.BlockSpec((tm, d), row), pl.BlockSpec((tm, o.shape[1]), row), _const_spec(wo.shape)]
    args = [h, o, wo]
    if bo is not None:
        in_specs.append(_const_spec(bo.shape))
        args.append(bo)
    in_specs += [_const_spec(gf.shape), _const_spec(wg.shape), _const_spec(wu.shape), _const_spec(wd.shape),
                 _const_spec(gp.shape), _const_spec(wpg.shape), pl.BlockSpec((tm, p.shape[1]), row),
                 _const_spec(wpp.shape)]
    args += [gf, wg, wu, wd, gp, wpg, p, wpp]
    if gn is not None:
        in_specs.append(_const_spec(gn.shape))
        args.append(gn)
    return pl.pallas_call(
        functools.partial(_post_kernel, has_bias=bo is not None, final_norm=gn is not None, ffn_chunk=ffn_chunk),
        grid=(t // tm,),
        in_specs=in_specs,
        out_specs=pl.BlockSpec((tm, d), row),
        out_shape=jax.ShapeDtypeStruct((t, d), F32),
        scratch_shapes=[pltpu.VMEM((tm, wg.shape[1]), BF16)],
        compiler_params=_params("parallel"),
        name="post_mixer",
    )(*args)


def _rope_tables(pos):
    inv = ROPE_THETA ** (-jnp.arange(HALF, dtype=F32) / HALF)
    ang = pos.astype(F32)[:, None] * inv[None, :]
    cos, sin = jnp.cos(ang), jnp.sin(ang)
    reps = LANES // HEAD_DIM
    return jnp.tile(cos, (1, 2 * reps)), jnp.tile(jnp.concatenate([-sin, sin], axis=1), (1, reps))


def _ffn_chunk(ffn):
    for c in (512, 256, 128):
        if ffn % c == 0:
            return c
    return ffn


def kernel(x_prompt, x_sample, cache_swa_k, cache_swa_v, cache_fox_k, cache_fox_v, cache_fox_logf, state_conv, page_table, p_prompt, p_sample, norm_mix, norm_ffn, norm_ple, norm_final, swa_w_qkv, swa_b_qkv, swa_w_o, swa_b_o, swa_sinks, fox_w_qkv, fox_w_f, fox_b_f, fox_w_o, conv_w_in, conv_w, conv_w_out, ffn_w_gate, ffn_w_up, ffn_w_down, ple_w_gate, ple_w_proj):
    batch, seq, d = x_prompt.shape
    db, n_new, _ = x_sample.shape
    depth = norm_mix.shape[0]
    tp, ts = batch * seq, db * n_new
    assert d == FOX_HEADS * HEAD_DIM == SWA_Q_HEADS * HEAD_DIM
    assert cache_swa_k.shape[2] == WINDOW and seq % WINDOW == 0 and n_new <= SUBLANES
    tm_p = min(512, seq)
    tm_s = ts

    hp = x_prompt.reshape(tp, d)
    hs = x_sample.reshape(ts, d)
    row = lambda a: a.reshape(1, -1)

    cos_p, sin_p = _rope_tables(jnp.arange(seq))
    cos_s, sin_s = _rope_tables(jnp.tile(PAST_LEN + jnp.arange(n_new), db))

    nqw = SWA_Q_HEADS * HEAD_DIM
    nkw = SWA_KV_HEADS * HEAD_DIM
    qperm = np.arange(nqw).reshape(SWA_KV_HEADS, SWA_GROUP, HEAD_DIM).transpose(1, 0, 2).reshape(-1)
    qkv_perm = np.concatenate([qperm, np.arange(nqw, nqw + 2 * nkw)])
    sink_rows = np.repeat(np.arange(SWA_Q_HEADS), SUBLANES)

    outs = {name: [] for name in ("swa_kp", "swa_vp", "swa_ks", "swa_vs", "fox_kp", "fox_vp", "fox_lp",
                                  "fox_ks", "fox_vs", "fox_ls", "conv_p", "conv_s")}
    for i in range(depth):
        j, kind = i // N_MIXERS, i % N_MIXERS
        g_mix = row(norm_mix[i])
        if kind == 0:
            w = swa_w_qkv[j][:, qkv_perm].astype(BF16)
            b = row(swa_b_qkv[j][qkv_perm])
            wo = swa_w_o[j][qperm, :].astype(BF16)
            bo = row(swa_b_o[j])
            qp, kp, vp = _swa_proj(hp, g_mix, w, b, cos_p, sin_p, tm=tm_p, q_dtype=BF16)
            op = _swa_prompt_attn(qp, kp, vp, swa_sinks[j], batch=batch, seq=seq)
            qs, ks, vs = _swa_proj(hs, g_mix, w, b, cos_s, sin_s, tm=tm_s, q_dtype=F32)
            pad_rows = lambda a, n: jnp.pad(a.reshape(db, n_new, -1), ((0, 0), (0, n - n_new), (0, 0)))
            ck = cache_swa_k[j].reshape(db, WINDOW, nkw)
            cv = cache_swa_v[j].reshape(db, WINDOW, nkw)
            o8 = _swa_sample_attn(pad_rows(qs, SUBLANES), pad_rows(ks, BF16_ROWS), pad_rows(vs, BF16_ROWS), ck, cv,
                                  swa_sinks[j][sink_rows].reshape(-1, 1), n_new=n_new, bb=min(16, db))
            os_ = o8[:, :n_new].reshape(ts, nqw)
            kp4 = kp.reshape(batch, seq, SWA_KV_HEADS, HEAD_DIM)
            vp4 = vp.reshape(batch, seq, SWA_KV_HEADS, HEAD_DIM)
            outs["swa_kp"].append(kp4[:, seq - WINDOW:])
            outs["swa_vp"].append(vp4[:, seq - WINDOW:])
            new_k = ks.reshape(db, n_new, SWA_KV_HEADS, HEAD_DIM)
            new_v = vs.reshape(db, n_new, SWA_KV_HEADS, HEAD_DIM)
            outs["swa_ks"].append(jnp.concatenate([cache_swa_k[j][:, n_new:], new_k], axis=1))
            outs["swa_vs"].append(jnp.concatenate([cache_swa_v[j][:, n_new:], new_v], axis=1))
        elif kind == 1:
            w = fox_w_qkv[j].astype(BF16)
            wf = jnp.tile(fox_w_f[j], (1, LANES // FOX_HEADS)).astype(BF16)
            bf = row(jnp.tile(fox_b_f[j], LANES // FOX_HEADS))
            wo, bo = fox_w_o[j].astype(BF16), None
            kp, vp, lp, qa, ka, vb = _fox_proj_prompt(hp, g_mix, w, wf, bf, tm=min(256, seq), seq=seq)
            op = _fox_flash(qa, ka, vb, batch=batch, seq=seq, tq=min(512, seq))
            outs["fox_kp"].append(kp.reshape(batch, seq, FOX_HEADS, HEAD_DIM))
            outs["fox_vp"].append(vp.reshape(batch, seq, FOX_HEADS, HEAD_DIM))
            outs["fox_lp"].append(lp.reshape(batch, seq, FOX_HEADS))

            n_pool = cache_fox_k.shape[1]
            n_pages = page_table.shape[1]
            past = n_pages * PAGE_SIZE
            logf_pages = cache_fox_logf[j].reshape(n_pool, PAGE_SIZE * FOX_HEADS // LANES, LANES)
            dpast = _fox_dpast(logf_pages, page_table).reshape(db, past, FOX_HEADS)
            dt = dpast.transpose(0, 2, 1)
            dlast_rows = jnp.repeat(dpast[:, past - 1, :], n_new, axis=0)
            qs, ks, vs, ls, dnew = _fox_proj_sample(hs, g_mix, w, wf, bf, dlast_rows, n_new=n_new)
            dn = dnew.reshape(db, n_new, FOX_HEADS)
            dq_col = dn.reshape(db, n_new * FOX_HEADS, 1)
            bias_new = jnp.pad(jnp.tile(dn.transpose(0, 2, 1), (1, n_new, 1)),
                               ((0, 0), (0, 0), (0, BF16_ROWS - n_new)))
            pad16 = lambda a: jnp.pad(a.reshape(db, n_new, d), ((0, 0), (0, BF16_ROWS - n_new), (0, 0)))
            os_ = _fox_sample_attn(cache_fox_k[j].reshape(n_pool, PAGE_SIZE, d),
                                   cache_fox_v[j].reshape(n_pool, PAGE_SIZE, d),
                                   page_table, dt, qs.reshape(db, n_new, d), pad16(ks), pad16(vs),
                                   dq_col, bias_new, n_new=n_new).reshape(ts, d)
            outs["fox_ks"].append(ks.reshape(db, n_new, FOX_HEADS, HEAD_DIM))
            outs["fox_vs"].append(vs.reshape(db, n_new, FOX_HEADS, HEAD_DIM))
            outs["fox_ls"].append(ls.reshape(db, n_new, FOX_HEADS))
        else:
            w = conv_w_in[j].astype(BF16)
            wo, bo = conv_w_out[j].astype(BF16), None
            op, up = _conv_proj(hp, g_mix, w, conv_w[j], None, tm=tm_p, seq=seq)
            st = state_conv[j]
            keep = CONV_WIDTH - 1
            s1 = jnp.pad(st[:, keep - 1:], ((0, 0), (0, n_new - 1), (0, 0))).reshape(ts, d)
            s2 = jnp.pad(st, ((0, 0), (0, n_new - keep), (0, 0))).reshape(ts, d)
            os_, us = _conv_proj(hs, g_mix, w, conv_w[j], (s1, s2), tm=tm_s, seq=n_new)
            outs["conv_p"].append(up[:, SUBLANES - keep:])
            outs["conv_s"].append(us.reshape(db, n_new, d)[:, n_new - keep:])

        wg, wu, wd = ffn_w_gate[i].astype(BF16), ffn_w_up[i].astype(BF16), ffn_w_down[i].astype(BF16)
        wpg, wpp = ple_w_gate[i].astype(BF16), ple_w_proj[i].astype(BF16)
        gn = row(norm_final) if i == depth - 1 else None
        common = dict(ffn_chunk=_ffn_chunk(wg.shape[1]))
        hp = _post(hp, op, wo, bo, row(norm_ffn[i]), wg, wu, wd, row(norm_ple[i]), wpg,
                   p_prompt[i].reshape(tp, -1), wpp, gn, tm=tm_p, **common)
        hs = _post(hs, os_, wo, bo, row(norm_ffn[i]), wg, wu, wd, row(norm_ple[i]), wpg,
                   p_sample[i].reshape(ts, -1), wpp, gn, tm=tm_s, **common)

    st = lambda name: jnp.stack(outs[name])
    return (hp.reshape(batch, seq, d), hs.reshape(db, n_new, d),
            st("swa_kp"), st("swa_vp"), st("swa_ks"), st("swa_vs"),
            st("fox_kp"), st("fox_vp"), st("fox_lp"),
            st("fox_ks"), st("fox_vs"), st("fox_ls"),
            st("conv_p"), st("conv_s"))
```

```python
import functools

import numpy as np
import jax
import jax.numpy as jnp
from jax import lax
from jax.experimental import pallas as pl
from jax.experimental.pallas import tpu as pltpu

F32 = jnp.float32
BF16 = jnp.bfloat16

HEAD_DIM = 64
HALF = HEAD_DIM // 2
SWA_Q_HEADS = 16
SWA_KV_HEADS = 4
SWA_GROUP = SWA_Q_HEADS // SWA_KV_HEADS
WINDOW = 128
FOX_HEADS = 16
PAGE_SIZE = 128
PAST_LEN = 8192
N_MIXERS = 3
CONV_WIDTH = 3
ROPE_THETA = 10000.0
EPS = 1e-6
NEG = -1e30
SCALE = HEAD_DIM ** -0.5

LANES = 128
SUBLANES = 8
BF16_ROWS = 16
VMEM_LIMIT = 56 * 1024 * 1024
AUG_COL = HEAD_DIM
FOX_PAGES_PER_STEP = 8
LOG2E = 1.4426950408889634


def _params(*sem):
    return pltpu.CompilerParams(dimension_semantics=sem, vmem_limit_bytes=VMEM_LIMIT)


def _const_spec(shape):
    nd = len(shape)
    return pl.BlockSpec(shape, lambda *_: (0,) * nd, pipeline_mode=pl.Buffered(1))


def _rms(x, g):
    ms = jnp.mean(x * x, axis=-1, keepdims=True)
    return x * lax.rsqrt(ms + EPS) * g


def _dot(a, b):
    return jnp.dot(a, b, preferred_element_type=F32)


def _dot_nt(a, b):
    return lax.dot_general(a, b, (((1,), (1,)), ((), ())), preferred_element_type=F32)


def _dot_exact(a, b):
    return jnp.dot(a, b, preferred_element_type=F32, precision=lax.Precision.HIGHEST)


def _sigmoid(x):
    return 1.0 / (1.0 + jnp.exp(-x))


def _log_sigmoid(z):
    return jnp.minimum(z, 0.0) - jnp.log1p(jnp.exp(-jnp.abs(z)))


def _swa_proj_kernel(x_ref, g_ref, w_ref, b_ref, cos_ref, sin_ref, q_ref, k_ref, v_ref):
    xn = _rms(x_ref[...], g_ref[...]).astype(BF16)
    qkv = _dot(xn, w_ref[...]) + b_ref[...]
    nq, nk = q_ref.shape[1], k_ref.shape[1]
    cos, sin = cos_ref[...], sin_ref[...]
    lane = lax.broadcasted_iota(jnp.int32, cos.shape, 1)
    first_half = (lane % HEAD_DIM) < HALF

    def rope(xs):
        sw = jnp.where(first_half, pltpu.roll(xs, LANES - HALF, 1), pltpu.roll(xs, HALF, 1))
        return xs * cos + sw * sin

    for j in range(nq // LANES):
        q_ref[:, j * LANES:(j + 1) * LANES] = rope(qkv[:, j * LANES:(j + 1) * LANES]).astype(q_ref.dtype)
    for j in range(nk // LANES):
        k_ref[:, j * LANES:(j + 1) * LANES] = rope(qkv[:, nq + j * LANES:nq + (j + 1) * LANES])
    v_ref[...] = qkv[:, nq + nk:]


def _swa_proj(x, g, w, b, cos_t, sin_t, *, tm, q_dtype):
    t, d = x.shape
    nq = SWA_Q_HEADS * HEAD_DIM
    nk = SWA_KV_HEADS * HEAD_DIM
    n_pos_blocks = cos_t.shape[0] // tm
    return pl.pallas_call(
        _swa_proj_kernel,
        grid=(t // tm,),
        in_specs=[
            pl.BlockSpec((tm, d), lambda i: (i, 0)),
            _const_spec((1, d)),
            _const_spec(w.shape),
            _const_spec((1, w.shape[1])),
            pl.BlockSpec((tm, LANES), lambda i: (i % n_pos_blocks, 0)),
            pl.BlockSpec((tm, LANES), lambda i: (i % n_pos_blocks, 0)),
        ],
        out_specs=[
            pl.BlockSpec((tm, nq), lambda i: (i, 0)),
            pl.BlockSpec((tm, nk), lambda i: (i, 0)),
            pl.BlockSpec((tm, nk), lambda i: (i, 0)),
        ],
        out_shape=[
            jax.ShapeDtypeStruct((t, nq), q_dtype),
            jax.ShapeDtypeStruct((t, nk), F32),
            jax.ShapeDtypeStruct((t, nk), F32),
        ],
        compiler_params=_params("parallel"),
        name="swa_proj",
    )(x, g, w, b, cos_t, sin_t)


def _swa_prompt_kernel(sink_ref, q_ref, kp_ref, kc_ref, vp_ref, vc_ref, o_ref):
    i = pl.program_id(1)
    w = WINDOW
    kband = jnp.concatenate([kp_ref[...], kc_ref[...]], axis=0)
    vband = jnp.concatenate([vp_ref[...], vc_ref[...]], axis=0)
    qi = lax.broadcasted_iota(jnp.int32, (w, 2 * w), 0)
    kj = lax.broadcasted_iota(jnp.int32, (w, 2 * w), 1)
    dist = qi + w - kj
    mask = (dist >= 0) & (dist < w) & ((kj >= w) | (i > 0))
    lane = lax.broadcasted_iota(jnp.int32, (2 * w, LANES), 1)
    low = lane < HEAD_DIM
    for kvp in range(SWA_KV_HEADS // 2):
        kt = kband[:, kvp * LANES:(kvp + 1) * LANES]
        vt = vband[:, kvp * LANES:(kvp + 1) * LANES]
        kz = [jnp.where(low, kt, 0.0).astype(BF16), jnp.where(low, 0.0, kt).astype(BF16)]
        vz = [jnp.where(low, vt, 0.0).astype(BF16), jnp.where(low, 0.0, vt).astype(BF16)]
        for g in range(SWA_GROUP):
            c0 = g * SWA_KV_HEADS * HEAD_DIM + kvp * LANES
            qt = q_ref[:, c0:c0 + LANES]
            o_pair = None
            for e in range(2):
                kv = kvp * 2 + e
                s = _dot_nt(qt, kz[e]) * SCALE
                s = jnp.where(mask, s, NEG)
                sk = sink_ref[kv * SWA_GROUP + g]
                m = jnp.maximum(jnp.max(s, axis=-1, keepdims=True), sk)
                ex = jnp.exp(s - m)
                den = jnp.sum(ex, axis=-1, keepdims=True) + jnp.exp(sk - m)
                p = (ex / den).astype(BF16)
                pv = _dot(p, vz[e])
                o_pair = pv if o_pair is None else o_pair + pv
            o_ref[:, c0:c0 + LANES] = o_pair.astype(o_ref.dtype)


def _swa_prompt_attn(q, k, v, sinks, *, batch, seq):
    nb = seq // WINDOW
    nq, nk = q.shape[1], k.shape[1]
    cur = lambda b, i: (b * nb + i, 0)
    prev = lambda b, i: (b * nb + jnp.maximum(i - 1, 0), 0)
    return pl.pallas_call(
        _swa_prompt_kernel,
        grid=(batch, nb),
        in_specs=[
            pl.BlockSpec(memory_space=pltpu.SMEM),
            pl.BlockSpec((WINDOW, nq), cur),
            pl.BlockSpec((WINDOW, nk), prev),
            pl.BlockSpec((WINDOW, nk), cur),
            pl.BlockSpec((WINDOW, nk), prev),
            pl.BlockSpec((WINDOW, nk), cur),
        ],
        out_specs=pl.BlockSpec((WINDOW, nq), cur),
        out_shape=jax.ShapeDtypeStruct(q.shape, BF16),
        compiler_params=_params("parallel", "parallel"),
        name="swa_prompt_attn",
    )(sinks, q, k, k, v, v)


def _swa_sample_kernel(sink_ref, q_ref, kn_ref, vn_ref, ck_ref, cv_ref, o_ref, *, n_new):
    bb = q_ref.shape[0]
    buf = ck_ref.shape[1]
    rows = SWA_Q_HEADS * SUBLANES
    kvw = SWA_KV_HEADS * HEAD_DIM
    lane = lax.broadcasted_iota(jnp.int32, (SUBLANES, kvw), 1)
    head_mask = [(lane // HEAD_DIM) == kv for kv in range(SWA_KV_HEADS)]
    t = lax.broadcasted_iota(jnp.int32, (rows, 1), 0) % SUBLANES
    valid_c = lax.broadcasted_iota(jnp.int32, (rows, buf), 1) > t
    tn = lax.broadcasted_iota(jnp.int32, (rows, BF16_ROWS), 1)
    valid_n = (tn <= t) & (tn < n_new)
    sk = sink_ref[...]

    def body(b, carry):
        q8 = q_ref[b]
        pieces = []
        for kv in range(SWA_KV_HEADS):
            for g in range(SWA_GROUP):
                pieces.append(jnp.where(head_mask[kv], q8[:, g * kvw:(g + 1) * kvw], 0.0))
        qbd = jnp.concatenate(pieces, axis=0).astype(BF16)
        s_c = jnp.where(valid_c, _dot_nt(qbd, ck_ref[b].astype(BF16)) * SCALE, NEG)
        s_n = jnp.where(valid_n, _dot_nt(qbd, kn_ref[b].astype(BF16)) * SCALE, NEG)
        m = jnp.maximum(jnp.maximum(jnp.max(s_c, axis=-1, keepdims=True),
                                    jnp.max(s_n, axis=-1, keepdims=True)), sk)
        e_c = jnp.exp(s_c - m)
        e_n = jnp.exp(s_n - m)
        den = jnp.sum(e_c, axis=-1, keepdims=True) + jnp.sum(e_n, axis=-1, keepdims=True) + jnp.exp(sk - m)
        o = (_dot((e_c / den).astype(BF16), cv_ref[b].astype(BF16))
             + _dot((e_n / den).astype(BF16), vn_ref[b].astype(BF16)))
        for g in range(SWA_GROUP):
            acc = None
            for kv in range(SWA_KV_HEADS):
                r0 = (kv * SWA_GROUP + g) * SUBLANES
                part = jnp.where(head_mask[kv], o[r0:r0 + SUBLANES], 0.0)
                acc = part if acc is None else acc + part
            o_ref[b, :, g * kvw:(g + 1) * kvw] = acc
        return carry

    lax.fori_loop(0, bb, body, 0)


def _swa_sample_attn(q8, kn16, vn16, cache_k, cache_v, sink_col, *, n_new, bb):
    db, buf, kvw = cache_k.shape
    nq = q8.shape[2]
    return pl.pallas_call(
        functools.partial(_swa_sample_kernel, n_new=n_new),
        grid=(db // bb,),
        in_specs=[
            _const_spec(sink_col.shape),
            pl.BlockSpec((bb, SUBLANES, nq), lambda i: (i, 0, 0)),
            pl.BlockSpec((bb, BF16_ROWS, kvw), lambda i: (i, 0, 0)),
            pl.BlockSpec((bb, BF16_ROWS, kvw), lambda i: (i, 0, 0)),
            pl.BlockSpec((bb, buf, kvw), lambda i: (i, 0, 0)),
            pl.BlockSpec((bb, buf, kvw), lambda i: (i, 0, 0)),
        ],
        out_specs=pl.BlockSpec((bb, SUBLANES, nq), lambda i: (i, 0, 0)),
        out_shape=jax.ShapeDtypeStruct(q8.shape, F32),
        compiler_params=_params("parallel"),
        name="swa_sample_attn",
    )(sink_col, q8, kn16, vn16, cache_k, cache_v)


def _fox_logf(xn, wf_ref, bf_ref):
    return _log_sigmoid(_dot(xn, wf_ref[...]) + bf_ref[...])


def _fox_proj_prompt_kernel(x_ref, g_ref, w_ref, wf_ref, bf_ref, pq_ref, pk_ref, oq_ref, ok_ref,
                            k_ref, v_ref, lf_ref, qa_ref, ka_ref, vt_ref, carry_ref, *, tiles_per_seq):
    i = pl.program_id(0)
    tm, d = x_ref.shape

    @pl.when(i % tiles_per_seq == 0)
    def _():
        carry_ref[...] = jnp.zeros_like(carry_ref)

    xn = _rms(x_ref[...], g_ref[...]).astype(BF16)
    qkv = _dot(xn, w_ref[...])
    vt_ref[...] = qkv[:, 2 * d:].T.astype(BF16)
    logf = _fox_logf(xn, wf_ref, bf_ref)
    lf_ref[...] = logf[:, :FOX_HEADS]

    r = lax.broadcasted_iota(jnp.int32, (tm, tm), 0)
    c = lax.broadcasted_iota(jnp.int32, (tm, tm), 1)
    ltri = jnp.where(c <= r, 1.0, 0.0).astype(F32)
    dcum = _dot_exact(ltri, logf) + carry_ref[...]
    carry_ref[...] = dcum[tm - 1:tm, :]

    d2 = dcum * LOG2E
    hi = d2.astype(BF16).astype(F32)
    r1 = d2 - hi
    mid = r1.astype(BF16).astype(F32)
    lo = r1 - mid
    grp = (lax.broadcasted_iota(jnp.int32, (tm, LANES), 1) // FOX_HEADS) % 3
    split = jnp.where(grp == 0, hi, jnp.where(grp == 1, mid, lo)).astype(BF16)
    aug_q = _dot(split, pq_ref[...]) + oq_ref[...]
    aug_k = _dot(split, pk_ref[...]) + ok_ref[...]

    low = lax.broadcasted_iota(jnp.int32, (tm, LANES), 1) < HEAD_DIM
    for h in range(FOX_HEADS):
        p, e = h // 2, h % 2
        qp = qkv[:, p * LANES:(p + 1) * LANES] * (SCALE * LOG2E)
        kp = qkv[:, d + p * LANES:d + (p + 1) * LANES]
        vp = qkv[:, 2 * d + p * LANES:2 * d + (p + 1) * LANES]
        if e == 1:
            qp = pltpu.roll(qp, HEAD_DIM, 1)
            kp = pltpu.roll(kp, HEAD_DIM, 1)
            vp = pltpu.roll(vp, HEAD_DIM, 1)
        qa_ref[:, h * LANES:(h + 1) * LANES] = jnp.where(low, qp, aug_q[:, h * LANES:(h + 1) * LANES]).astype(BF16)
        ka_ref[:, h * LANES:(h + 1) * LANES] = jnp.where(low, kp, aug_k[:, h * LANES:(h + 1) * LANES]).astype(BF16)
        k_ref[pl.ds(h, tm, stride=FOX_HEADS), :] = kp[:, :HEAD_DIM]
        v_ref[pl.ds(h, tm, stride=FOX_HEADS), :] = vp[:, :HEAD_DIM]


def _fox_aug_constants():
    pq = np.zeros((LANES, FOX_HEADS * LANES), np.float32)
    pk = np.zeros((LANES, FOX_HEADS * LANES), np.float32)
    oq = np.zeros((1, FOX_HEADS * LANES), np.float32)
    ok = np.zeros((1, FOX_HEADS * LANES), np.float32)
    for h in range(FOX_HEADS):
        base = h * LANES + AUG_COL
        for c in range(3):
            pk[c * FOX_HEADS + h, base + c] = -1.0
            oq[0, base + c] = 1.0
            pq[c * FOX_HEADS + h, base + 3 + c] = 1.0
            ok[0, base + 3 + c] = 1.0
    return jnp.asarray(pq, BF16), jnp.asarray(pk, BF16), jnp.asarray(oq), jnp.asarray(ok)


def _fox_proj_prompt(x, g, w, wf, bf, *, tm, seq):
    t, d = x.shape
    pq, pk, oq, ok = _fox_aug_constants()
    aw = FOX_HEADS * LANES
    row = lambda i: (i, 0)
    return pl.pallas_call(
        functools.partial(_fox_proj_prompt_kernel, tiles_per_seq=seq // tm),
        grid=(t // tm,),
        in_specs=[
            pl.BlockSpec((tm, d), row),
            _const_spec((1, d)),
            _const_spec(w.shape),
            _const_spec(wf.shape),
            _const_spec(bf.shape),
            _const_spec(pq.shape),
            _const_spec(pk.shape),
            _const_spec(oq.shape),
            _const_spec(ok.shape),
        ],
        out_specs=[
            pl.BlockSpec((tm * FOX_HEADS, HEAD_DIM), row),
            pl.BlockSpec((tm * FOX_HEADS, HEAD_DIM), row),
            pl.BlockSpec((tm, FOX_HEADS), row),
            pl.BlockSpec((tm, aw), row),
            pl.BlockSpec((tm, aw), row),
            pl.BlockSpec((d, tm), lambda i: (0, i)),
        ],
        out_shape=[
            jax.ShapeDtypeStruct((t * FOX_HEADS, HEAD_DIM), F32),
            jax.ShapeDtypeStruct((t * FOX_HEADS, HEAD_DIM), F32),
            jax.ShapeDtypeStruct((t, FOX_HEADS), F32),
            jax.ShapeDtypeStruct((t, aw), BF16),
            jax.ShapeDtypeStruct((t, aw), BF16),
            jax.ShapeDtypeStruct((d, t), BF16),
        ],
        scratch_shapes=[pltpu.VMEM((1, LANES), F32)],
        compiler_params=_params("arbitrary"),
        name="fox_proj_prompt",
    )(x, g, w, wf, bf, pq, pk, oq, ok)


def _fox_proj_sample_kernel(x_ref, g_ref, w_ref, wf_ref, bf_ref, q_ref, k_ref, v_ref, lf_ref):
    d = k_ref.shape[1]
    xn = _rms(x_ref[...], g_ref[...]).astype(BF16)
    qkv = _dot(xn, w_ref[...])
    q_ref[...] = qkv[:, :d]
    k_ref[...] = qkv[:, d:2 * d]
    v_ref[...] = qkv[:, 2 * d:]
    lf_ref[...] = _fox_logf(xn, wf_ref, bf_ref)[:, :FOX_HEADS]


def _fox_proj_sample(x, g, w, wf, bf):
    t, d = x.shape
    return pl.pallas_call(
        _fox_proj_sample_kernel,
        grid=(1,),
        in_specs=[
            _const_spec((t, d)),
            _const_spec((1, d)),
            _const_spec(w.shape),
            _const_spec(wf.shape),
            _const_spec(bf.shape),
        ],
        out_specs=[
            _const_spec((t, d)), _const_spec((t, d)), _const_spec((t, d)),
            _const_spec((t, FOX_HEADS)),
        ],
        out_shape=[
            jax.ShapeDtypeStruct((t, d), F32),
            jax.ShapeDtypeStruct((t, d), F32),
            jax.ShapeDtypeStruct((t, d), F32),
            jax.ShapeDtypeStruct((t, FOX_HEADS), F32),
        ],
        compiler_params=_params("arbitrary"),
        name="fox_proj_sample",
    )(x, g, w, wf, bf)


def _fox_flash_kernel(qi_ref, kj_ref, q_ref, k_ref, vt_ref, o_ref, m_ref, l_ref, acc_ref):
    step = pl.program_id(2)
    i, j = qi_ref[step], kj_ref[step]
    tq, tk = q_ref.shape[0], k_ref.shape[0]

    @pl.when(j == 0)
    def _():
        m_ref[...] = jnp.full_like(m_ref, NEG)
        l_ref[...] = jnp.zeros_like(l_ref)
        acc_ref[...] = jnp.zeros_like(acc_ref)

    def update(masked):
        for e in range(2):
            st = _dot_nt(k_ref[:, e * LANES:(e + 1) * LANES], q_ref[:, e * LANES:(e + 1) * LANES])
            if masked:
                key = j * tk + lax.broadcasted_iota(jnp.int32, (tk, tq), 0)
                qry = i * tq + lax.broadcasted_iota(jnp.int32, (tk, tq), 1)
                st = jnp.where(key > qry, NEG, st)
            m_prev = m_ref[e]
            m_new = jnp.maximum(m_prev, jnp.max(st, axis=0, keepdims=True))
            alpha = jnp.exp2(m_prev - m_new)
            p = jnp.exp2(st - m_new)
            l_ref[e] = alpha * l_ref[e] + jnp.sum(p, axis=0, keepdims=True)
            m_ref[e] = m_new
            rows = slice(e * HEAD_DIM, (e + 1) * HEAD_DIM)
            acc_ref[rows, :] = acc_ref[rows, :] * alpha + _dot(vt_ref[rows, :], p.astype(BF16))

    crosses_diagonal = (j + 1) * tk - 1 > i * tq
    pl.when(crosses_diagonal)(functools.partial(update, True))
    pl.when(jnp.logical_not(crosses_diagonal))(functools.partial(update, False))

    @pl.when(j == ((i + 1) * tq - 1) // tk)
    def _():
        ot = jnp.concatenate([acc_ref[e * HEAD_DIM:(e + 1) * HEAD_DIM, :] / l_ref[e] for e in range(2)], axis=0)
        o_ref[...] = ot.T.astype(o_ref.dtype)


def _fox_flash(qa, ka, vt, *, batch, seq, tq, tk):
    nq, nk = seq // tq, seq // tk
    pairs = [(i, j) for i in range(nq) for j in range(((i + 1) * tq - 1) // tk + 1)]
    qi = jnp.asarray([p[0] for p in pairs], jnp.int32)
    kj = jnp.asarray([p[1] for p in pairs], jnp.int32)
    d, t = vt.shape
    grid_spec = pltpu.PrefetchScalarGridSpec(
        num_scalar_prefetch=2,
        grid=(batch, FOX_HEADS // 2, len(pairs)),
        in_specs=[
            pl.BlockSpec((tq, 2 * LANES), lambda b, p, s, qi, kj: (b * nq + qi[s], p)),
            pl.BlockSpec((tk, 2 * LANES), lambda b, p, s, qi, kj: (b * nk + kj[s], p)),
            pl.BlockSpec((LANES, tk), lambda b, p, s, qi, kj: (p, b * nk + kj[s])),
        ],
        out_specs=pl.BlockSpec((tq, LANES), lambda b, p, s, qi, kj: (b * nq + qi[s], p)),
        scratch_shapes=[
            pltpu.VMEM((2, 1, tq), F32),
            pltpu.VMEM((2, 1, tq), F32),
            pltpu.VMEM((LANES, tq), F32),
        ],
    )
    return pl.pallas_call(
        _fox_flash_kernel,
        grid_spec=grid_spec,
        out_shape=jax.ShapeDtypeStruct((t, d), BF16),
        compiler_params=_params("parallel", "parallel", "arbitrary"),
        name="fox_flash",
    )(qi, kj, qa, ka, vt)


def _fox_sample_kernel(pt_ref, *refs):
    n = FOX_PAGES_PER_STEP
    k_refs, v_refs, lf_refs = refs[:n], refs[n:2 * n], refs[2 * n:3 * n]
    q_ref, kn_ref, vn_ref, lfn_ref, o_ref, m_ref, l_ref, acc_ref, carry_ref = refs[3 * n:]
    j = pl.program_id(1)
    rows = q_ref.shape[1]

    @pl.when(j == 0)
    def _():
        m_ref[...] = jnp.full_like(m_ref, NEG)
        l_ref[...] = jnp.zeros_like(l_ref)
        acc_ref[...] = jnp.zeros_like(acc_ref)
        carry_ref[...] = jnp.zeros_like(carry_ref)

    x = jnp.concatenate([r[0] for r in lf_refs], axis=0)
    xr = x.shape[0]
    li = lax.broadcasted_iota(jnp.int32, (LANES, LANES), 0)
    lj = lax.broadcasted_iota(jnp.int32, (LANES, LANES), 1)
    same_head = (li % FOX_HEADS) == (lj % FOX_HEADS)
    within = jnp.where(same_head & (li <= lj), 1.0, 0.0).astype(F32)
    total = jnp.where(same_head, 1.0, 0.0).astype(F32)
    ri = lax.broadcasted_iota(jnp.int32, (xr, xr), 0)
    rj = lax.broadcasted_iota(jnp.int32, (xr, xr), 1)
    above = jnp.where(rj < ri, 1.0, 0.0).astype(F32)
    tot = _dot_exact(x, total)
    dpast = _dot_exact(x, within) + _dot_exact(above, tot) + carry_ref[...]
    carry_ref[...] = carry_ref[...] + jnp.sum(tot, axis=0, keepdims=True)

    qm = (q_ref[0] * SCALE).astype(BF16)
    own = ((lax.broadcasted_iota(jnp.int32, (rows, LANES), 1) % FOX_HEADS)
           == (lax.broadcasted_iota(jnp.int32, (rows, LANES), 0) % FOX_HEADS))
    chunks_per_page = k_refs[0].shape[1] // LANES
    chunks = []
    for pg in range(n):
        st = _dot_nt(qm, k_refs[pg][0].astype(BF16))
        for a in range(chunks_per_page):
            r0 = pg * chunks_per_page + a
            chunks.append(jnp.where(own, st[:, a * LANES:(a + 1) * LANES] - dpast[r0:r0 + 1, :], NEG))
    s = jnp.concatenate(chunks, axis=1)
    m_prev = m_ref[...]
    m_new = jnp.maximum(m_prev, jnp.max(s, axis=-1, keepdims=True))
    alpha = jnp.exp(m_prev - m_new)
    p = jnp.exp(s - m_new)
    l_ref[...] = alpha * l_ref[...] + jnp.sum(p, axis=-1, keepdims=True)
    m_ref[...] = m_new
    vcat = jnp.concatenate([r[0].astype(BF16) for r in v_refs], axis=0)
    acc_ref[...] = acc_ref[...] * alpha + _dot(p.astype(BF16), vcat)

    @pl.when(j == pl.num_programs(1) - 1)
    def _():
        ci = lax.broadcasted_iota(jnp.int32, (rows, rows), 0)
        cj = lax.broadcasted_iota(jnp.int32, (rows, rows), 1)
        head_match = (ci % FOX_HEADS) == (cj % FOX_HEADS)
        prefix = jnp.where(head_match & (ci <= cj), 1.0, 0.0).astype(F32)
        lfn = jnp.broadcast_to(lfn_ref[0], (SUBLANES, rows))
        d_new = carry_ref[:, :rows] + _dot_exact(lfn, prefix)[:1, :]
        dq = jnp.sum(jnp.where(ci == cj, jnp.broadcast_to(d_new, (rows, rows)), 0.0), axis=1, keepdims=True)
        s_n = _dot_nt(qm, kn_ref[0].astype(BF16)) - d_new + dq
        s_n = jnp.where(head_match & (cj // FOX_HEADS <= ci // FOX_HEADS), s_n, NEG)
        m_prev = m_ref[...] + dq
        m_new = jnp.maximum(m_prev, jnp.max(s_n, axis=-1, keepdims=True))
        alpha = jnp.exp(m_prev - m_new)
        p_n = jnp.exp(s_n - m_new)
        l_fin = alpha * l_ref[...] + jnp.sum(p_n, axis=-1, keepdims=True)
        o_ref[0] = (acc_ref[...] * alpha + _dot(p_n.astype(BF16), vn_ref[0].astype(BF16))) / l_fin


def _fox_sample_attn(k_pages, v_pages, logf_pages, page_table, q, kn, vn, lfn):
    db, n_pages = page_table.shape
    n = FOX_PAGES_PER_STEP
    steps = n_pages // n
    rows = q.shape[1]

    def page_spec(k, shape):
        return pl.BlockSpec((1,) + shape, lambda b, j, pt: (pt[b, j * n + k], 0, 0))

    def per_seq(shape):
        return pl.BlockSpec((1,) + shape, lambda b, j, pt: (b, 0, 0))

    kv_shape = k_pages.shape[1:]
    grid_spec = pltpu.PrefetchScalarGridSpec(
        num_scalar_prefetch=1,
        grid=(db, steps),
        in_specs=([page_spec(k, kv_shape) for k in range(n)] + [page_spec(k, kv_shape) for k in range(n)]
                  + [page_spec(k, logf_pages.shape[1:]) for k in range(n)]
                  + [per_seq(q.shape[1:]), per_seq(kn.shape[1:]), per_seq(vn.shape[1:]), per_seq(lfn.shape[1:])]),
        out_specs=per_seq(q.shape[1:]),
        scratch_shapes=[
            pltpu.VMEM((rows, 1), F32),
            pltpu.VMEM((rows, 1), F32),
            pltpu.VMEM((rows, HEAD_DIM), F32),
            pltpu.VMEM((1, LANES), F32),
        ],
    )
    return pl.pallas_call(
        _fox_sample_kernel,
        grid_spec=grid_spec,
        out_shape=jax.ShapeDtypeStruct(q.shape, F32),
        compiler_params=_params("parallel", "arbitrary"),
        name="fox_sample_attn",
    )(page_table, *([k_pages] * n), *([v_pages] * n), *([logf_pages] * n), q, kn, vn, lfn)


def _conv_proj_kernel(*refs, seq, has_state):
    if has_state:
        x_ref, g_ref, w_ref, cw_ref, s1_ref, s2_ref, y_ref, u_ref, carry_ref = refs
    else:
        x_ref, g_ref, w_ref, cw_ref, y_ref, u_ref, carry_ref = refs
    i = pl.program_id(0)
    tm, d = x_ref.shape

    @pl.when(i == 0)
    def _():
        carry_ref[...] = jnp.zeros_like(carry_ref)

    xn = _rms(x_ref[...], g_ref[...]).astype(BF16)
    proj = _dot(xn, w_ref[...])
    gb, gc, hh = proj[:, :d], proj[:, d:2 * d], proj[:, 2 * d:]
    u = gc * hh
    row = lax.broadcasted_iota(jnp.int32, (tm, 1), 0)
    pos = (i * tm + row) % seq
    prev_tile = carry_ref[...]
    um1 = jnp.where(row == 0, prev_tile[7:8, :], pltpu.roll(u, 1, 0))
    um2 = jnp.where(row == 0, prev_tile[6:7, :], jnp.where(row == 1, prev_tile[7:8, :], pltpu.roll(u, 2, 0)))
    um1 = jnp.where(pos >= 1, um1, 0.0)
    um2 = jnp.where(pos >= 2, um2, 0.0)
    if has_state:
        um1 = um1 + s1_ref[...]
        um2 = um2 + s2_ref[...]
    cw = cw_ref[...]
    conv = cw[0:1, :] * um2 + cw[1:2, :] * um1 + cw[2:3, :] * u
    y_ref[...] = (gb * conv).astype(y_ref.dtype)
    carry_ref[...] = u[tm - SUBLANES:, :]
    if has_state:
        u_ref[...] = u
    else:
        u_ref[0] = u[tm - SUBLANES:, :]


def _conv_proj(x, g, w, cw, state_rows, *, tm, seq):
    t, d = x.shape
    has_state = state_rows is not None
    row = lambda i: (i, 0)
    in_specs = [pl.BlockSpec((tm, d), row), _const_spec((1, d)), _const_spec(w.shape), _const_spec(cw.shape)]
    args = [x, g, w, cw]
    if has_state:
        in_specs += [pl.BlockSpec((tm, d), row), pl.BlockSpec((tm, d), row)]
        args += list(state_rows)
        u_spec = pl.BlockSpec((tm, d), row)
        u_shape = jax.ShapeDtypeStruct((t, d), F32)
    else:
        tiles_per_seq = seq // tm
        u_spec = pl.BlockSpec((1, SUBLANES, d), lambda i: (i // tiles_per_seq, 0, 0))
        u_shape = jax.ShapeDtypeStruct((t // seq, SUBLANES, d), F32)
    return pl.pallas_call(
        functools.partial(_conv_proj_kernel, seq=seq, has_state=has_state),
        grid=(t // tm,),
        in_specs=in_specs,
        out_specs=[pl.BlockSpec((tm, d), row), u_spec],
        out_shape=[jax.ShapeDtypeStruct((t, d), BF16), u_shape],
        scratch_shapes=[pltpu.VMEM((SUBLANES, d), F32)],
        compiler_params=_params("arbitrary"),
        name="conv_proj",
    )(*args)


def _post_kernel(*refs, has_bias, final_norm, ffn_chunk):
    refs = list(refs)
    h_ref, o_ref, wo_ref = refs[:3]
    refs = refs[3:]
    bo_ref = refs.pop(0) if has_bias else None
    gf_ref, wg_ref, wu_ref, wd_ref, gp_ref, wpg_ref, p_ref, wpp_ref = refs[:8]
    refs = refs[8:]
    gn_ref = refs.pop(0) if final_norm else None
    out_ref, act_ref = refs

    y = _dot(o_ref[...].astype(BF16), wo_ref[...])
    if has_bias:
        y = y + bo_ref[...]
    h1 = h_ref[...] + y
    hn = _rms(h1, gf_ref[...]).astype(BF16)
    ffn = wg_ref.shape[1]
    for c in range(ffn // ffn_chunk):
        sl = slice(c * ffn_chunk, (c + 1) * ffn_chunk)
        gate = _dot(hn, wg_ref[:, sl])
        up = _dot(hn, wu_ref[:, sl])
        act_ref[:, sl] = (gate * _sigmoid(gate) * up).astype(BF16)
    h2 = h1 + _dot(act_ref[...], wd_ref[...])
    pg = _sigmoid(_dot(_rms(h2, gp_ref[...]).astype(BF16), wpg_ref[...]))
    h3 = h2 + pg * _dot(p_ref[...].astype(BF16), wpp_ref[...])
    if final_norm:
        h3 = _rms(h3, gn_ref[...])
    out_ref[...] = h3


def _post(h, o, wo, bo, gf, wg, wu, wd, gp, wpg, p, wpp, gn, *, tm, ffn_chunk):
    t, d = h.shape
    row = lambda i: (i, 0)
    in_specs = [pl.BlockSpec((tm, d), row), pl.BlockSpec((tm, o.shape[1]), row), _const_spec(wo.shape)]
    args = [h, o, wo]
    if bo is not None:
        in_specs.append(_const_spec(bo.shape))
        args.append(bo)
    in_specs += [_const_spec(gf.shape), _const_spec(wg.shape), _const_spec(wu.shape), _const_spec(wd.shape),
                 _const_spec(gp.shape), _const_spec(wpg.shape), pl.BlockSpec((tm, p.shape[1]), row),
                 _const_spec(wpp.shape)]
    args += [gf, wg, wu, wd, gp, wpg, p, wpp]
    if gn is not None:
        in_specs.append(_const_spec(gn.shape))
        args.append(gn)
    return pl.pallas_call(
        functools.partial(_post_kernel, has_bias=bo is not None, final_norm=gn is not None, ffn_chunk=ffn_chunk),
        grid=(t // tm,),
        in_specs=in_specs,
        out_specs=pl.BlockSpec((tm, d), row),
        out_shape=jax.ShapeDtypeStruct((t, d), F32),
        scratch_shapes=[pltpu.VMEM((tm, wg.shape[1]), BF16)],
        compiler_params=_params("parallel"),
        name="post_mixer",
    )(*args)


def _rope_tables(pos):
    inv = ROPE_THETA ** (-jnp.arange(HALF, dtype=F32) / HALF)
    ang = pos.astype(F32)[:, None] * inv[None, :]
    cos, sin = jnp.cos(ang), jnp.sin(ang)
    reps = LANES // HEAD_DIM
    return jnp.tile(cos, (1, 2 * reps)), jnp.tile(jnp.concatenate([-sin, sin], axis=1), (1, reps))


def _ffn_chunk(ffn):
    for c in (512, 256, 128):
        if ffn % c == 0:
            return c
    return ffn


def kernel(x_prompt, x_sample, cache_swa_k, cache_swa_v, cache_fox_k, cache_fox_v, cache_fox_logf, state_conv, page_table, p_prompt, p_sample, norm_mix, norm_ffn, norm_ple, norm_final, swa_w_qkv, swa_b_qkv, swa_w_o, swa_b_o, swa_sinks, fox_w_qkv, fox_w_f, fox_b_f, fox_w_o, conv_w_in, conv_w, conv_w_out, ffn_w_gate, ffn_w_up, ffn_w_down, ple_w_gate, ple_w_proj):
    batch, seq, d = x_prompt.shape
    db, n_new, _ = x_sample.shape
    depth = norm_mix.shape[0]
    tp, ts = batch * seq, db * n_new
    assert d == FOX_HEADS * HEAD_DIM == SWA_Q_HEADS * HEAD_DIM
    assert cache_swa_k.shape[2] == WINDOW and seq % WINDOW == 0 and n_new <= SUBLANES
    tm_p = min(512, seq)
    tm_s = ts

    hp = x_prompt.reshape(tp, d)
    hs = x_sample.reshape(ts, d)
    row = lambda a: a.reshape(1, -1)

    cos_p, sin_p = _rope_tables(jnp.arange(seq))
    cos_s, sin_s = _rope_tables(jnp.tile(PAST_LEN + jnp.arange(n_new), db))

    nqw = SWA_Q_HEADS * HEAD_DIM
    nkw = SWA_KV_HEADS * HEAD_DIM
    qperm = np.arange(nqw).reshape(SWA_KV_HEADS, SWA_GROUP, HEAD_DIM).transpose(1, 0, 2).reshape(-1)
    qkv_perm = np.concatenate([qperm, np.arange(nqw, nqw + 2 * nkw)])
    sink_rows = np.repeat(np.arange(SWA_Q_HEADS), SUBLANES)

    outs = {name: [] for name in ("swa_kp", "swa_vp", "swa_ks", "swa_vs", "fox_kp", "fox_vp", "fox_lp",
                                  "fox_ks", "fox_vs", "fox_ls", "conv_p", "conv_s")}
    for i in range(depth):
        j, kind = i // N_MIXERS, i % N_MIXERS
        g_mix = row(norm_mix[i])
        if kind == 0:
            w = swa_w_qkv[j][:, qkv_perm].astype(BF16)
            b = row(swa_b_qkv[j][qkv_perm])
            wo = swa_w_o[j][qperm, :].astype(BF16)
            bo = row(swa_b_o[j])
            qp, kp, vp = _swa_proj(hp, g_mix, w, b, cos_p, sin_p, tm=tm_p, q_dtype=BF16)
            op = _swa_prompt_attn(qp, kp, vp, swa_sinks[j], batch=batch, seq=seq)
            qs, ks, vs = _swa_proj(hs, g_mix, w, b, cos_s, sin_s, tm=tm_s, q_dtype=F32)
            pad_rows = lambda a, n: jnp.pad(a.reshape(db, n_new, -1), ((0, 0), (0, n - n_new), (0, 0)))
            ck = cache_swa_k[j].reshape(db, WINDOW, nkw)
            cv = cache_swa_v[j].reshape(db, WINDOW, nkw)
            o8 = _swa_sample_attn(pad_rows(qs, SUBLANES), pad_rows(ks, BF16_ROWS), pad_rows(vs, BF16_ROWS), ck, cv,
                                  swa_sinks[j][sink_rows].reshape(-1, 1), n_new=n_new, bb=min(16, db))
            os_ = o8[:, :n_new].reshape(ts, nqw)
            kp4 = kp.reshape(batch, seq, SWA_KV_HEADS, HEAD_DIM)
            vp4 = vp.reshape(batch, seq, SWA_KV_HEADS, HEAD_DIM)
            outs["swa_kp"].append(kp4[:, seq - WINDOW:])
            outs["swa_vp"].append(vp4[:, seq - WINDOW:])
            new_k = ks.reshape(db, n_new, SWA_KV_HEADS, HEAD_DIM)
            new_v = vs.reshape(db, n_new, SWA_KV_HEADS, HEAD_DIM)
            outs["swa_ks"].append(jnp.concatenate([cache_swa_k[j][:, n_new:], new_k], axis=1))
            outs["swa_vs"].append(jnp.concatenate([cache_swa_v[j][:, n_new:], new_v], axis=1))
        elif kind == 1:
            w = fox_w_qkv[j].astype(BF16)
            wf = jnp.tile(fox_w_f[j], (1, LANES // FOX_HEADS)).astype(BF16)
            bf = row(jnp.tile(fox_b_f[j], LANES // FOX_HEADS))
            wo, bo = fox_w_o[j].astype(BF16), None
            kp, vp, lp, qa, ka, vt = _fox_proj_prompt(hp, g_mix, w, wf, bf, tm=min(256, seq), seq=seq)
            op = _fox_flash(qa, ka, vt, batch=batch, seq=seq, tq=min(1024, seq), tk=min(512, seq))
            outs["fox_kp"].append(kp.reshape(batch, seq, FOX_HEADS, HEAD_DIM))
            outs["fox_vp"].append(vp.reshape(batch, seq, FOX_HEADS, HEAD_DIM))
            outs["fox_lp"].append(lp.reshape(batch, seq, FOX_HEADS))

            n_pool = cache_fox_k.shape[1]
            qs, ks, vs, ls = _fox_proj_sample(hs, g_mix, w, wf, bf)
            flat = PAGE_SIZE * FOX_HEADS
            by_row = lambda a: a.reshape(db, n_new * FOX_HEADS, HEAD_DIM)
            os_ = _fox_sample_attn(cache_fox_k[j].reshape(n_pool, flat, HEAD_DIM),
                                   cache_fox_v[j].reshape(n_pool, flat, HEAD_DIM),
                                   cache_fox_logf[j].reshape(n_pool, flat // LANES, LANES), page_table,
                                   by_row(qs), by_row(ks), by_row(vs),
                                   ls.reshape(db, 1, n_new * FOX_HEADS)).reshape(ts, d)
            outs["fox_ks"].append(ks.reshape(db, n_new, FOX_HEADS, HEAD_DIM))
            outs["fox_vs"].append(vs.reshape(db, n_new, FOX_HEADS, HEAD_DIM))
            outs["fox_ls"].append(ls.reshape(db, n_new, FOX_HEADS))
        else:
            w = conv_w_in[j].astype(BF16)
            wo, bo = conv_w_out[j].astype(BF16), None
            op, up = _conv_proj(hp, g_mix, w, conv_w[j], None, tm=tm_p, seq=seq)
            st = state_conv[j]
            keep = CONV_WIDTH - 1
            s1 = jnp.pad(st[:, keep - 1:], ((0, 0), (0, n_new - 1), (0, 0))).reshape(ts, d)
            s2 = jnp.pad(st, ((0, 0), (0, n_new - keep), (0, 0))).reshape(ts, d)
            os_, us = _conv_proj(hs, g_mix, w, conv_w[j], (s1, s2), tm=tm_s, seq=n_new)
            outs["conv_p"].append(up[:, SUBLANES - keep:])
            outs["conv_s"].append(us.reshape(db, n_new, d)[:, n_new - keep:])

        wg, wu, wd = ffn_w_gate[i].astype(BF16), ffn_w_up[i].astype(BF16), ffn_w_down[i].astype(BF16)
        wpg, wpp = ple_w_gate[i].astype(BF16), ple_w_proj[i].astype(BF16)
        gn = row(norm_final) if i == depth - 1 else None
        common = dict(ffn_chunk=_ffn_chunk(wg.shape[1]))
        hp = _post(hp, op, wo, bo, row(norm_ffn[i]), wg, wu, wd, row(norm_ple[i]), wpg,
                   p_prompt[i].reshape(tp, -1), wpp, gn, tm=tm_p, **common)
        hs = _post(hs, os_, wo, bo, row(norm_ffn[i]), wg, wu, wd, row(norm_ple[i]), wpg,
                   p_sample[i].reshape(ts, -1), wpp, gn, tm=tm_s, **common)

    st = lambda name: jnp.stack(outs[name])
    return (hp.reshape(batch, seq, d), hs.reshape(db, n_new, d),
            st("swa_kp"), st("swa_vp"), st("swa_ks"), st("swa_vs"),
            st("fox_kp"), st("fox_vp"), st("fox_lp"),
            st("fox_ks"), st("fox_vs"), st("fox_ls"),
            st("conv_p"), st("conv_s"))
```

```python
import functools

import numpy as np
import jax
import jax.numpy as jnp
from jax import lax
from jax.experimental import pallas as pl
from jax.experimental.pallas import tpu as pltpu

F32 = jnp.float32
BF16 = jnp.bfloat16

HEAD_DIM = 64
HALF = HEAD_DIM // 2
SWA_Q_HEADS = 16
SWA_KV_HEADS = 4
SWA_GROUP = SWA_Q_HEADS // SWA_KV_HEADS
WINDOW = 128
FOX_HEADS = 16
PAGE_SIZE = 128
PAST_LEN = 8192
N_MIXERS = 3
CONV_WIDTH = 3
ROPE_THETA = 10000.0
EPS = 1e-6
NEG = -1e30
SCALE = HEAD_DIM ** -0.5

LANES = 128
SUBLANES = 8
BF16_ROWS = 16
VMEM_LIMIT = 56 * 1024 * 1024
AUG_COL = HEAD_DIM
FOX_PAGES_PER_STEP = 16
LOG2E = 1.4426950408889634


def _params(*sem):
    return pltpu.CompilerParams(dimension_semantics=sem, vmem_limit_bytes=VMEM_LIMIT)


def _const_spec(shape):
    nd = len(shape)
    return pl.BlockSpec(shape, lambda *_: (0,) * nd, pipeline_mode=pl.Buffered(1))


def _rms(x, g):
    ms = jnp.mean(x * x, axis=-1, keepdims=True)
    return x * lax.rsqrt(ms + EPS) * g


def _dot(a, b):
    return jnp.dot(a, b, preferred_element_type=F32)


def _dot_nt(a, b):
    return lax.dot_general(a, b, (((1,), (1,)), ((), ())), preferred_element_type=F32)


def _dot_exact(a, b):
    return jnp.dot(a, b, preferred_element_type=F32, precision=lax.Precision.HIGHEST)


def _sigmoid(x):
    return 1.0 / (1.0 + jnp.exp(-x))


def _log_sigmoid(z):
    return jnp.minimum(z, 0.0) - jnp.log1p(jnp.exp(-jnp.abs(z)))


def _swa_proj_kernel(x_ref, g_ref, w_ref, b_ref, cos_ref, sin_ref, q_ref, k_ref, v_ref):
    xn = _rms(x_ref[...], g_ref[...]).astype(BF16)
    qkv = _dot(xn, w_ref[...]) + b_ref[...]
    nq, nk = q_ref.shape[1], k_ref.shape[1]
    cos, sin = cos_ref[...], sin_ref[...]
    lane = lax.broadcasted_iota(jnp.int32, cos.shape, 1)
    first_half = (lane % HEAD_DIM) < HALF

    def rope(xs):
        sw = jnp.where(first_half, pltpu.roll(xs, LANES - HALF, 1), pltpu.roll(xs, HALF, 1))
        return xs * cos + sw * sin

    for j in range(nq // LANES):
        q_ref[:, j * LANES:(j + 1) * LANES] = rope(qkv[:, j * LANES:(j + 1) * LANES]).astype(q_ref.dtype)
    for j in range(nk // LANES):
        k_ref[:, j * LANES:(j + 1) * LANES] = rope(qkv[:, nq + j * LANES:nq + (j + 1) * LANES])
    v_ref[...] = qkv[:, nq + nk:]


def _swa_proj(x, g, w, b, cos_t, sin_t, *, tm, q_dtype):
    t, d = x.shape
    nq = SWA_Q_HEADS * HEAD_DIM
    nk = SWA_KV_HEADS * HEAD_DIM
    n_pos_blocks = cos_t.shape[0] // tm
    return pl.pallas_call(
        _swa_proj_kernel,
        grid=(t // tm,),
        in_specs=[
            pl.BlockSpec((tm, d), lambda i: (i, 0)),
            _const_spec((1, d)),
            _const_spec(w.shape),
            _const_spec((1, w.shape[1])),
            pl.BlockSpec((tm, LANES), lambda i: (i % n_pos_blocks, 0)),
            pl.BlockSpec((tm, LANES), lambda i: (i % n_pos_blocks, 0)),
        ],
        out_specs=[
            pl.BlockSpec((tm, nq), lambda i: (i, 0)),
            pl.BlockSpec((tm, nk), lambda i: (i, 0)),
            pl.BlockSpec((tm, nk), lambda i: (i, 0)),
        ],
        out_shape=[
            jax.ShapeDtypeStruct((t, nq), q_dtype),
            jax.ShapeDtypeStruct((t, nk), F32),
            jax.ShapeDtypeStruct((t, nk), F32),
        ],
        compiler_params=_params("parallel"),
        name="swa_proj",
    )(x, g, w, b, cos_t, sin_t)


def _swa_prompt_kernel(sink_ref, q_ref, kp_ref, kc_ref, vp_ref, vc_ref, o_ref):
    i = pl.program_id(1)
    w = WINDOW
    kband = jnp.concatenate([kp_ref[...], kc_ref[...]], axis=0)
    vband = jnp.concatenate([vp_ref[...], vc_ref[...]], axis=0)
    qi = lax.broadcasted_iota(jnp.int32, (w, 2 * w), 0)
    kj = lax.broadcasted_iota(jnp.int32, (w, 2 * w), 1)
    dist = qi + w - kj
    mask = (dist >= 0) & (dist < w) & ((kj >= w) | (i > 0))
    lane = lax.broadcasted_iota(jnp.int32, (2 * w, LANES), 1)
    low = lane < HEAD_DIM
    for kvp in range(SWA_KV_HEADS // 2):
        kt = kband[:, kvp * LANES:(kvp + 1) * LANES]
        vt = vband[:, kvp * LANES:(kvp + 1) * LANES]
        kz = [jnp.where(low, kt, 0.0).astype(BF16), jnp.where(low, 0.0, kt).astype(BF16)]
        vz = [jnp.where(low, vt, 0.0).astype(BF16), jnp.where(low, 0.0, vt).astype(BF16)]
        for g in range(SWA_GROUP):
            c0 = g * SWA_KV_HEADS * HEAD_DIM + kvp * LANES
            qt = q_ref[:, c0:c0 + LANES]
            o_pair = None
            for e in range(2):
                kv = kvp * 2 + e
                s = _dot_nt(qt, kz[e]) * SCALE
                s = jnp.where(mask, s, NEG)
                sk = sink_ref[kv * SWA_GROUP + g]
                m = jnp.maximum(jnp.max(s, axis=-1, keepdims=True), sk)
                ex = jnp.exp(s - m)
                den = jnp.sum(ex, axis=-1, keepdims=True) + jnp.exp(sk - m)
                p = (ex / den).astype(BF16)
                pv = _dot(p, vz[e])
                o_pair = pv if o_pair is None else o_pair + pv
            o_ref[:, c0:c0 + LANES] = o_pair.astype(o_ref.dtype)


def _swa_prompt_attn(q, k, v, sinks, *, batch, seq):
    nb = seq // WINDOW
    nq, nk = q.shape[1], k.shape[1]
    cur = lambda b, i: (b * nb + i, 0)
    prev = lambda b, i: (b * nb + jnp.maximum(i - 1, 0), 0)
    return pl.pallas_call(
        _swa_prompt_kernel,
        grid=(batch, nb),
        in_specs=[
            pl.BlockSpec(memory_space=pltpu.SMEM),
            pl.BlockSpec((WINDOW, nq), cur),
            pl.BlockSpec((WINDOW, nk), prev),
            pl.BlockSpec((WINDOW, nk), cur),
            pl.BlockSpec((WINDOW, nk), prev),
            pl.BlockSpec((WINDOW, nk), cur),
        ],
        out_specs=pl.BlockSpec((WINDOW, nq), cur),
        out_shape=jax.ShapeDtypeStruct(q.shape, BF16),
        compiler_params=_params("parallel", "parallel"),
        name="swa_prompt_attn",
    )(sinks, q, k, k, v, v)


def _swa_sample_kernel(sink_ref, q_ref, kn_ref, vn_ref, ck_ref, cv_ref, o_ref, *, n_new):
    bb = q_ref.shape[0]
    buf = ck_ref.shape[1]
    rows = SWA_Q_HEADS * SUBLANES
    kvw = SWA_KV_HEADS * HEAD_DIM
    lane = lax.broadcasted_iota(jnp.int32, (SUBLANES, kvw), 1)
    head_mask = [(lane // HEAD_DIM) == kv for kv in range(SWA_KV_HEADS)]
    t = lax.broadcasted_iota(jnp.int32, (rows, 1), 0) % SUBLANES
    valid_c = lax.broadcasted_iota(jnp.int32, (rows, buf), 1) > t
    tn = lax.broadcasted_iota(jnp.int32, (rows, BF16_ROWS), 1)
    valid_n = (tn <= t) & (tn < n_new)
    sk = sink_ref[...]

    def body(b, carry):
        q8 = q_ref[b]
        pieces = []
        for kv in range(SWA_KV_HEADS):
            for g in range(SWA_GROUP):
                pieces.append(jnp.where(head_mask[kv], q8[:, g * kvw:(g + 1) * kvw], 0.0))
        qbd = jnp.concatenate(pieces, axis=0).astype(BF16)
        s_c = jnp.where(valid_c, _dot_nt(qbd, ck_ref[b].astype(BF16)) * SCALE, NEG)
        s_n = jnp.where(valid_n, _dot_nt(qbd, kn_ref[b].astype(BF16)) * SCALE, NEG)
        m = jnp.maximum(jnp.maximum(jnp.max(s_c, axis=-1, keepdims=True),
                                    jnp.max(s_n, axis=-1, keepdims=True)), sk)
        e_c = jnp.exp(s_c - m)
        e_n = jnp.exp(s_n - m)
        den = jnp.sum(e_c, axis=-1, keepdims=True) + jnp.sum(e_n, axis=-1, keepdims=True) + jnp.exp(sk - m)
        o = (_dot((e_c / den).astype(BF16), cv_ref[b].astype(BF16))
             + _dot((e_n / den).astype(BF16), vn_ref[b].astype(BF16)))
        for g in range(SWA_GROUP):
            acc = None
            for kv in range(SWA_KV_HEADS):
                r0 = (kv * SWA_GROUP + g) * SUBLANES
                part = jnp.where(head_mask[kv], o[r0:r0 + SUBLANES], 0.0)
                acc = part if acc is None else acc + part
            o_ref[b, :, g * kvw:(g + 1) * kvw] = acc
        return carry

    lax.fori_loop(0, bb, body, 0)


def _swa_sample_attn(q8, kn16, vn16, cache_k, cache_v, sink_col, *, n_new, bb):
    db, buf, kvw = cache_k.shape
    nq = q8.shape[2]
    return pl.pallas_call(
        functools.partial(_swa_sample_kernel, n_new=n_new),
        grid=(db // bb,),
        in_specs=[
            _const_spec(sink_col.shape),
            pl.BlockSpec((bb, SUBLANES, nq), lambda i: (i, 0, 0)),
            pl.BlockSpec((bb, BF16_ROWS, kvw), lambda i: (i, 0, 0)),
            pl.BlockSpec((bb, BF16_ROWS, kvw), lambda i: (i, 0, 0)),
            pl.BlockSpec((bb, buf, kvw), lambda i: (i, 0, 0)),
            pl.BlockSpec((bb, buf, kvw), lambda i: (i, 0, 0)),
        ],
        out_specs=pl.BlockSpec((bb, SUBLANES, nq), lambda i: (i, 0, 0)),
        out_shape=jax.ShapeDtypeStruct(q8.shape, F32),
        compiler_params=_params("parallel"),
        name="swa_sample_attn",
    )(sink_col, q8, kn16, vn16, cache_k, cache_v)


def _fox_logf(xn, wf_ref, bf_ref):
    return _log_sigmoid(_dot(xn, wf_ref[...]) + bf_ref[...])


def _fox_proj_prompt_kernel(x_ref, g_ref, w_ref, wf_ref, bf_ref, pq_ref, pk_ref, oq_ref, ok_ref,
                            k_ref, v_ref, lf_ref, qa_ref, ka_ref, vt_ref, carry_ref, *, tiles_per_seq):
    i = pl.program_id(0)
    tm, d = x_ref.shape

    @pl.when(i % tiles_per_seq == 0)
    def _():
        carry_ref[...] = jnp.zeros_like(carry_ref)

    xn = _rms(x_ref[...], g_ref[...]).astype(BF16)
    qkv = _dot(xn, w_ref[...])
    vt = qkv[:, 2 * d:].T
    k_ref[0] = qkv[:, d:2 * d].T
    v_ref[0] = vt
    vt_ref[...] = vt.astype(BF16)
    logf = _fox_logf(xn, wf_ref, bf_ref)
    lf_ref[0] = logf.T[:FOX_HEADS, :]

    r = lax.broadcasted_iota(jnp.int32, (tm, tm), 0)
    c = lax.broadcasted_iota(jnp.int32, (tm, tm), 1)
    ltri = jnp.where(c <= r, 1.0, 0.0).astype(F32)
    dcum = _dot_exact(ltri, logf) + carry_ref[...]
    carry_ref[...] = dcum[tm - 1:tm, :]

    d2 = dcum * LOG2E
    hi = d2.astype(BF16).astype(F32)
    r1 = d2 - hi
    mid = r1.astype(BF16).astype(F32)
    lo = r1 - mid
    grp = (lax.broadcasted_iota(jnp.int32, (tm, LANES), 1) // FOX_HEADS) % 3
    split = jnp.where(grp == 0, hi, jnp.where(grp == 1, mid, lo)).astype(BF16)
    aug_q = _dot(split, pq_ref[...]) + oq_ref[...]
    aug_k = _dot(split, pk_ref[...]) + ok_ref[...]

    low = lax.broadcasted_iota(jnp.int32, (tm, LANES), 1) < HEAD_DIM
    for h in range(FOX_HEADS):
        p, e = h // 2, h % 2
        qp = qkv[:, p * LANES:(p + 1) * LANES] * (SCALE * LOG2E)
        kp = qkv[:, d + p * LANES:d + (p + 1) * LANES]
        if e == 1:
            qp = pltpu.roll(qp, HEAD_DIM, 1)
            kp = pltpu.roll(kp, HEAD_DIM, 1)
        qa_ref[:, h * LANES:(h + 1) * LANES] = jnp.where(low, qp, aug_q[:, h * LANES:(h + 1) * LANES]).astype(BF16)
        ka_ref[:, h * LANES:(h + 1) * LANES] = jnp.where(low, kp, aug_k[:, h * LANES:(h + 1) * LANES]).astype(BF16)


def _fox_aug_constants():
    pq = np.zeros((LANES, FOX_HEADS * LANES), np.float32)
    pk = np.zeros((LANES, FOX_HEADS * LANES), np.float32)
    oq = np.zeros((1, FOX_HEADS * LANES), np.float32)
    ok = np.zeros((1, FOX_HEADS * LANES), np.float32)
    for h in range(FOX_HEADS):
        base = h * LANES + AUG_COL
        for c in range(3):
            pk[c * FOX_HEADS + h, base + c] = -1.0
            oq[0, base + c] = 1.0
            pq[c * FOX_HEADS + h, base + 3 + c] = 1.0
            ok[0, base + 3 + c] = 1.0
    return jnp.asarray(pq, BF16), jnp.asarray(pk, BF16), jnp.asarray(oq), jnp.asarray(ok)


def _fox_proj_prompt(x, g, w, wf, bf, *, tm, seq):
    t, d = x.shape
    pq, pk, oq, ok = _fox_aug_constants()
    aw = FOX_HEADS * LANES
    tiles_per_seq = seq // tm
    row = lambda i: (i, 0)
    by_seq = lambda i: (i // tiles_per_seq, 0, i % tiles_per_seq)
    return pl.pallas_call(
        functools.partial(_fox_proj_prompt_kernel, tiles_per_seq=tiles_per_seq),
        grid=(t // tm,),
        in_specs=[
            pl.BlockSpec((tm, d), row),
            _const_spec((1, d)),
            _const_spec(w.shape),
            _const_spec(wf.shape),
            _const_spec(bf.shape),
            _const_spec(pq.shape),
            _const_spec(pk.shape),
            _const_spec(oq.shape),
            _const_spec(ok.shape),
        ],
        out_specs=[
            pl.BlockSpec((1, d, tm), by_seq),
            pl.BlockSpec((1, d, tm), by_seq),
            pl.BlockSpec((1, FOX_HEADS, tm), by_seq),
            pl.BlockSpec((tm, aw), row),
            pl.BlockSpec((tm, aw), row),
            pl.BlockSpec((d, tm), lambda i: (0, i)),
        ],
        out_shape=[
            jax.ShapeDtypeStruct((t // seq, d, seq), F32),
            jax.ShapeDtypeStruct((t // seq, d, seq), F32),
            jax.ShapeDtypeStruct((t // seq, FOX_HEADS, seq), F32),
            jax.ShapeDtypeStruct((t, aw), BF16),
            jax.ShapeDtypeStruct((t, aw), BF16),
            jax.ShapeDtypeStruct((d, t), BF16),
        ],
        scratch_shapes=[pltpu.VMEM((1, LANES), F32)],
        compiler_params=_params("arbitrary"),
        name="fox_proj_prompt",
    )(x, g, w, wf, bf, pq, pk, oq, ok)


def _fox_proj_sample_kernel(x_ref, g_ref, w_ref, wf_ref, bf_ref, q_ref, k_ref, v_ref, lf_ref):
    d = k_ref.shape[1]
    xn = _rms(x_ref[...], g_ref[...]).astype(BF16)
    qkv = _dot(xn, w_ref[...])
    q_ref[...] = qkv[:, :d]
    k_ref[...] = qkv[:, d:2 * d]
    v_ref[...] = qkv[:, 2 * d:]
    lf_ref[...] = _fox_logf(xn, wf_ref, bf_ref)[:, :FOX_HEADS]


def _fox_proj_sample(x, g, w, wf, bf):
    t, d = x.shape
    return pl.pallas_call(
        _fox_proj_sample_kernel,
        grid=(1,),
        in_specs=[
            _const_spec((t, d)),
            _const_spec((1, d)),
            _const_spec(w.shape),
            _const_spec(wf.shape),
            _const_spec(bf.shape),
        ],
        out_specs=[
            _const_spec((t, d)), _const_spec((t, d)), _const_spec((t, d)),
            _const_spec((t, FOX_HEADS)),
        ],
        out_shape=[
            jax.ShapeDtypeStruct((t, d), F32),
            jax.ShapeDtypeStruct((t, d), F32),
            jax.ShapeDtypeStruct((t, d), F32),
            jax.ShapeDtypeStruct((t, FOX_HEADS), F32),
        ],
        compiler_params=_params("arbitrary"),
        name="fox_proj_sample",
    )(x, g, w, wf, bf)


def _fox_flash_kernel(qi_ref, kj_ref, q_ref, k_ref, vt_ref, o_ref, m_ref, l_ref, acc_ref):
    step = pl.program_id(2)
    i, j = qi_ref[step], kj_ref[step]
    tq, tk = q_ref.shape[0], k_ref.shape[0]

    @pl.when(j == 0)
    def _():
        m_ref[...] = jnp.full_like(m_ref, NEG)
        l_ref[...] = jnp.zeros_like(l_ref)
        acc_ref[...] = jnp.zeros_like(acc_ref)

    def update(masked):
        for e in range(2):
            st = _dot_nt(k_ref[:, e * LANES:(e + 1) * LANES], q_ref[:, e * LANES:(e + 1) * LANES])
            if masked:
                key = j * tk + lax.broadcasted_iota(jnp.int32, (tk, tq), 0)
                qry = i * tq + lax.broadcasted_iota(jnp.int32, (tk, tq), 1)
                st = jnp.where(key > qry, NEG, st)
            m_prev = m_ref[e]
            m_new = jnp.maximum(m_prev, jnp.max(st, axis=0, keepdims=True))
            alpha = jnp.exp2(m_prev - m_new)
            p = jnp.exp2(st - m_new)
            l_ref[e] = alpha * l_ref[e] + jnp.sum(p, axis=0, keepdims=True)
            m_ref[e] = m_new
            rows = slice(e * HEAD_DIM, (e + 1) * HEAD_DIM)
            acc_ref[rows, :] = acc_ref[rows, :] * alpha + _dot(vt_ref[rows, :], p.astype(BF16))

    crosses_diagonal = (j + 1) * tk - 1 > i * tq
    pl.when(crosses_diagonal)(functools.partial(update, True))
    pl.when(jnp.logical_not(crosses_diagonal))(functools.partial(update, False))

    @pl.when(j == ((i + 1) * tq - 1) // tk)
    def _():
        ot = jnp.concatenate([acc_ref[e * HEAD_DIM:(e + 1) * HEAD_DIM, :] / l_ref[e] for e in range(2)], axis=0)
        o_ref[...] = ot.T.astype(o_ref.dtype)


def _fox_flash(qa, ka, vt, *, batch, seq, tq, tk):
    nq, nk = seq // tq, seq // tk
    pairs = [(i, j) for i in range(nq) for j in range(((i + 1) * tq - 1) // tk + 1)]
    qi = jnp.asarray([p[0] for p in pairs], jnp.int32)
    kj = jnp.asarray([p[1] for p in pairs], jnp.int32)
    d, t = vt.shape
    grid_spec = pltpu.PrefetchScalarGridSpec(
        num_scalar_prefetch=2,
        grid=(batch, FOX_HEADS // 2, len(pairs)),
        in_specs=[
            pl.BlockSpec((tq, 2 * LANES), lambda b, p, s, qi, kj: (b * nq + qi[s], p)),
            pl.BlockSpec((tk, 2 * LANES), lambda b, p, s, qi, kj: (b * nk + kj[s], p)),
            pl.BlockSpec((LANES, tk), lambda b, p, s, qi, kj: (p, b * nk + kj[s])),
        ],
        out_specs=pl.BlockSpec((tq, LANES), lambda b, p, s, qi, kj: (b * nq + qi[s], p)),
        scratch_shapes=[
            pltpu.VMEM((2, 1, tq), F32),
            pltpu.VMEM((2, 1, tq), F32),
            pltpu.VMEM((LANES, tq), F32),
        ],
    )
    return pl.pallas_call(
        _fox_flash_kernel,
        grid_spec=grid_spec,
        out_shape=jax.ShapeDtypeStruct((t, d), BF16),
        compiler_params=_params("parallel", "parallel", "arbitrary"),
        name="fox_flash",
    )(qi, kj, qa, ka, vt)


def _fox_sample_kernel(pt_ref, *refs, n_new):
    n = FOX_PAGES_PER_STEP
    k_refs, v_refs, lf_refs = refs[:n], refs[n:2 * n], refs[2 * n:3 * n]
    q_ref, kn_ref, vn_ref, lfn_ref, o_ref, m_ref, l_ref, acc_ref, carry_ref = refs[3 * n:]
    j = pl.program_id(1)

    @pl.when(j == 0)
    def _():
        m_ref[...] = jnp.full_like(m_ref, NEG)
        l_ref[...] = jnp.zeros_like(l_ref)
        acc_ref[...] = jnp.zeros_like(acc_ref)
        carry_ref[...] = jnp.zeros_like(carry_ref)

    ui = lax.broadcasted_iota(jnp.int32, (PAGE_SIZE, PAGE_SIZE), 0)
    uj = lax.broadcasted_iota(jnp.int32, (PAGE_SIZE, PAGE_SIZE), 1)
    upper = jnp.where(ui <= uj, 1.0, 0.0).astype(F32)
    within = _dot_exact(jnp.concatenate([r[0, 0] for r in lf_refs], axis=0), upper)
    carry = carry_ref[...]
    parts = []
    for pg in range(n):
        dpg = within[pg * FOX_HEADS:(pg + 1) * FOX_HEADS] + carry
        carry = dpg[:, PAGE_SIZE - 1:PAGE_SIZE]
        parts.append(dpg)
    carry_ref[...] = carry
    dpast = jnp.concatenate(parts, axis=1)

    def keys_on_lanes(page_refs, h):
        return jnp.concatenate([r[0, 0, h] for r in page_refs], axis=1).astype(BF16)

    heads = range(FOX_HEADS)
    q = [(q_ref[0, h] * SCALE).astype(BF16) for h in heads]

    def per_head_rows(x):
        return jnp.concatenate([jnp.broadcast_to(x[h:h + 1, :], (SUBLANES, x.shape[1])) for h in heads], axis=0)

    def head_block(x, h):
        return x[h * SUBLANES:(h + 1) * SUBLANES]

    s = jnp.concatenate([_dot(q[h], keys_on_lanes(k_refs, h)) for h in heads], axis=0) - per_head_rows(dpast)
    m_prev = m_ref[...]
    m_new = jnp.maximum(m_prev, jnp.max(s, axis=-1, keepdims=True))
    alpha = jnp.exp(m_prev - m_new)
    p = jnp.exp(s - m_new)
    l_ref[...] = alpha * l_ref[...] + jnp.sum(p, axis=-1, keepdims=True)
    m_ref[...] = m_new
    pv = jnp.concatenate([_dot_nt(head_block(p, h).astype(BF16), keys_on_lanes(v_refs, h)) for h in heads], axis=0)
    acc_ref[...] = acc_ref[...] * alpha + pv

    @pl.when(j == pl.num_programs(1) - 1)
    def _():
        rows = lfn_ref.shape[1]
        ti = lax.broadcasted_iota(jnp.int32, (rows, rows), 0)
        tj = lax.broadcasted_iota(jnp.int32, (rows, rows), 1)
        last_key = jnp.where(lax.broadcasted_iota(jnp.int32, (rows, PAGE_SIZE), 1) == PAGE_SIZE - 1, 1.0, 0.0)
        d_last = lax.dot_general(last_key.astype(F32), parts[-1], (((1,), (1,)), ((), ())),
                                 precision=lax.Precision.HIGHEST, preferred_element_type=F32)
        d_new = d_last + _dot_exact(jnp.where(tj <= ti, 1.0, 0.0).astype(F32), lfn_ref[0])
        d_new_t = lax.dot_general(jnp.where(ti == tj, 1.0, 0.0).astype(F32), d_new, (((1,), (1,)), ((), ())),
                                  precision=lax.Precision.HIGHEST, preferred_element_type=F32)
        dq = jnp.concatenate([d_new[:SUBLANES, h:h + 1] for h in heads], axis=0)
        s_n = (jnp.concatenate([_dot_nt(q[h], kn_ref[0, h].astype(BF16)) for h in heads], axis=0)
               - per_head_rows(d_new_t) + dq)
        t_q = lax.broadcasted_iota(jnp.int32, s_n.shape, 0) % SUBLANES
        t_k = lax.broadcasted_iota(jnp.int32, s_n.shape, 1)
        s_n = jnp.where((t_k <= t_q) & (t_k < n_new), s_n, NEG)
        m_old = m_ref[...] + dq
        m_fin = jnp.maximum(m_old, jnp.max(s_n, axis=-1, keepdims=True))
        a_fin = jnp.exp(m_old - m_fin)
        p_n = jnp.exp(s_n - m_fin)
        l_fin = a_fin * l_ref[...] + jnp.sum(p_n, axis=-1, keepdims=True)
        pv_n = jnp.concatenate([_dot(head_block(p_n, h).astype(BF16), vn_ref[0, h].astype(BF16)) for h in heads],
                               axis=0)
        o = (acc_ref[...] * a_fin + pv_n) / l_fin
        for h in heads:
            o_ref[0, h] = head_block(o, h)


def _fox_sample_attn(kt_pages, vt_pages, lt_pages, page_table, q4, kn4, vn4, lfn, *, layer, n_new):
    db, n_pages = page_table.shape
    n = FOX_PAGES_PER_STEP
    steps = n_pages // n

    def pool_spec(k, arr):
        nd = arr.ndim - 2
        return pl.BlockSpec((1, 1) + arr.shape[2:], lambda b, j, pt: (layer, pt[b, j * n + k]) + (0,) * nd)

    def per_seq(arr):
        nd = arr.ndim - 1
        return pl.BlockSpec((1,) + arr.shape[1:], lambda b, j, pt: (b,) + (0,) * nd)

    grid_spec = pltpu.PrefetchScalarGridSpec(
        num_scalar_prefetch=1,
        grid=(db, steps),
        in_specs=([pool_spec(k, kt_pages) for k in range(n)] + [pool_spec(k, vt_pages) for k in range(n)]
                  + [pool_spec(k, lt_pages) for k in range(n)]
                  + [per_seq(q4), per_seq(kn4), per_seq(vn4), per_seq(lfn)]),
        out_specs=per_seq(q4),
        scratch_shapes=[
            pltpu.VMEM((FOX_HEADS * SUBLANES, 1), F32),
            pltpu.VMEM((FOX_HEADS * SUBLANES, 1), F32),
            pltpu.VMEM((FOX_HEADS * SUBLANES, HEAD_DIM), F32),
            pltpu.VMEM((FOX_HEADS, 1), F32),
        ],
    )
    return pl.pallas_call(
        functools.partial(_fox_sample_kernel, n_new=n_new),
        grid_spec=grid_spec,
        out_shape=jax.ShapeDtypeStruct(q4.shape, F32),
        compiler_params=_params("parallel", "arbitrary"),
        name="fox_sample_attn",
    )(page_table, *([kt_pages] * n), *([vt_pages] * n), *([lt_pages] * n), q4, kn4, vn4, lfn)


def _conv_proj_kernel(*refs, seq, has_state):
    if has_state:
        x_ref, g_ref, w_ref, cw_ref, s1_ref, s2_ref, y_ref, u_ref, carry_ref = refs
    else:
        x_ref, g_ref, w_ref, cw_ref, y_ref, u_ref, carry_ref = refs
    i = pl.program_id(0)
    tm, d = x_ref.shape

    @pl.when(i == 0)
    def _():
        carry_ref[...] = jnp.zeros_like(carry_ref)

    xn = _rms(x_ref[...], g_ref[...]).astype(BF16)
    proj = _dot(xn, w_ref[...])
    gb, gc, hh = proj[:, :d], proj[:, d:2 * d], proj[:, 2 * d:]
    u = gc * hh
    row = lax.broadcasted_iota(jnp.int32, (tm, 1), 0)
    pos = (i * tm + row) % seq
    prev_tile = carry_ref[...]
    um1 = jnp.where(row == 0, prev_tile[7:8, :], pltpu.roll(u, 1, 0))
    um2 = jnp.where(row == 0, prev_tile[6:7, :], jnp.where(row == 1, prev_tile[7:8, :], pltpu.roll(u, 2, 0)))
    um1 = jnp.where(pos >= 1, um1, 0.0)
    um2 = jnp.where(pos >= 2, um2, 0.0)
    if has_state:
        um1 = um1 + s1_ref[...]
        um2 = um2 + s2_ref[...]
    cw = cw_ref[...]
    conv = cw[0:1, :] * um2 + cw[1:2, :] * um1 + cw[2:3, :] * u
    y_ref[...] = (gb * conv).astype(y_ref.dtype)
    carry_ref[...] = u[tm - SUBLANES:, :]
    if has_state:
        u_ref[...] = u
    else:
        u_ref[0] = u[tm - SUBLANES:, :]


def _conv_proj(x, g, w, cw, state_rows, *, tm, seq):
    t, d = x.shape
    has_state = state_rows is not None
    row = lambda i: (i, 0)
    in_specs = [pl.BlockSpec((tm, d), row), _const_spec((1, d)), _const_spec(w.shape), _const_spec(cw.shape)]
    args = [x, g, w, cw]
    if has_state:
        in_specs += [pl.BlockSpec((tm, d), row), pl.BlockSpec((tm, d), row)]
        args += list(state_rows)
        u_spec = pl.BlockSpec((tm, d), row)
        u_shape = jax.ShapeDtypeStruct((t, d), F32)
    else:
        tiles_per_seq = seq // tm
        u_spec = pl.BlockSpec((1, SUBLANES, d), lambda i: (i // tiles_per_seq, 0, 0))
        u_shape = jax.ShapeDtypeStruct((t // seq, SUBLANES, d), F32)
    return pl.pallas_call(
        functools.partial(_conv_proj_kernel, seq=seq, has_state=has_state),
        grid=(t // tm,),
        in_specs=in_specs,
        out_specs=[pl.BlockSpec((tm, d), row), u_spec],
        out_shape=[jax.ShapeDtypeStruct((t, d), BF16), u_shape],
        scratch_shapes=[pltpu.VMEM((SUBLANES, d), F32)],
        compiler_params=_params("arbitrary"),
        name="conv_proj",
    )(*args)


def _post_kernel(*refs, has_bias, final_norm, ffn_chunk):
    refs = list(refs)
    h_ref, o_ref, wo_ref = refs[:3]
    refs = refs[3:]
    bo_ref = refs.pop(0) if has_bias else None
    gf_ref, wg_ref, wu_ref, wd_ref, gp_ref, wpg_ref, p_ref, wpp_ref = refs[:8]
    refs = refs[8:]
    gn_ref = refs.pop(0) if final_norm else None
    out_ref, act_ref = refs

    y = _dot(o_ref[...].astype(BF16), wo_ref[...])
    if has_bias:
        y = y + bo_ref[...]
    h1 = h_ref[...] + y
    hn = _rms(h1, gf_ref[...]).astype(BF16)
    ffn = wg_ref.shape[1]
    for c in range(ffn // ffn_chunk):
        sl = slice(c * ffn_chunk, (c + 1) * ffn_chunk)
        gate = _dot(hn, wg_ref[:, sl])
        up = _dot(hn, wu_ref[:, sl])
        act_ref[:, sl] = (gate * _sigmoid(gate) * up).astype(BF16)
    h2 = h1 + _dot(act_ref[...], wd_ref[...])
    pg = _sigmoid(_dot(_rms(h2, gp_ref[...]).astype(BF16), wpg_ref[...]))
    h3 = h2 + pg * _dot(p_ref[...].astype(BF16), wpp_ref[...])
    if final_norm:
        h3 = _rms(h3, gn_ref[...])
    out_ref[...] = h3


def _post(h, o, wo, bo, gf, wg, wu, wd, gp, wpg, p, wpp, gn, *, tm, ffn_chunk):
    t, d = h.shape
    row = lambda i: (i, 0)
    in_specs = [pl.BlockSpec((tm, d), row), pl.BlockSpec((tm, o.shape[1]), row), _const_spec(wo.shape)]
    args = [h, o, wo]
    if bo is not None:
        in_specs.append(_const_spec(bo.shape))
        args.append(bo)
    in_specs += [_const_spec(gf.shape), _const_spec(wg.shape), _const_spec(wu.shape), _const_spec(wd.shape),
                 _const_spec(gp.shape), _const_spec(wpg.shape), pl.BlockSpec((tm, p.shape[1]), row),
                 _const_spec(wpp.shape)]
    args += [gf, wg, wu, wd, gp, wpg, p, wpp]
    if gn is not None:
        in_specs.append(_const_spec(gn.shape))
        args.append(gn)
    return pl.pallas_call(
        functools.partial(_post_kernel, has_bias=bo is not None, final_norm=gn is not None, ffn_chunk=ffn_chunk),
        grid=(t // tm,),
        in_specs=in_specs,
        out_specs=pl.BlockSpec((tm, d), row),
        out_shape=jax.ShapeDtypeStruct((t, d), F32),
        scratch_shapes=[pltpu.VMEM((tm, wg.shape[1]), BF16)],
        compiler_params=_params("parallel"),
        name="post_mixer",
    )(*args)


def _rope_tables(pos):
    inv = ROPE_THETA ** (-jnp.arange(HALF, dtype=F32) / HALF)
    ang = pos.astype(F32)[:, None] * inv[None, :]
    cos, sin = jnp.cos(ang), jnp.sin(ang)
    reps = LANES // HEAD_DIM
    return jnp.tile(cos, (1, 2 * reps)), jnp.tile(jnp.concatenate([-sin, sin], axis=1), (1, reps))


def _ffn_chunk(ffn):
    for c in (512, 256, 128):
        if ffn % c == 0:
            return c
    return ffn


def kernel(x_prompt, x_sample, cache_swa_k, cache_swa_v, cache_fox_k, cache_fox_v, cache_fox_logf, state_conv, page_table, p_prompt, p_sample, norm_mix, norm_ffn, norm_ple, norm_final, swa_w_qkv, swa_b_qkv, swa_w_o, swa_b_o, swa_sinks, fox_w_qkv, fox_w_f, fox_b_f, fox_w_o, conv_w_in, conv_w, conv_w_out, ffn_w_gate, ffn_w_up, ffn_w_down, ple_w_gate, ple_w_proj):
    batch, seq, d = x_prompt.shape
    db, n_new, _ = x_sample.shape
    depth = norm_mix.shape[0]
    tp, ts = batch * seq, db * n_new
    assert d == FOX_HEADS * HEAD_DIM == SWA_Q_HEADS * HEAD_DIM
    assert cache_swa_k.shape[2] == WINDOW and seq % WINDOW == 0 and n_new <= SUBLANES
    tm_p = min(512, seq)
    tm_s = ts

    hp = x_prompt.reshape(tp, d)
    hs = x_sample.reshape(ts, d)
    row = lambda a: a.reshape(1, -1)

    cos_p, sin_p = _rope_tables(jnp.arange(seq))
    cos_s, sin_s = _rope_tables(jnp.tile(PAST_LEN + jnp.arange(n_new), db))

    nqw = SWA_Q_HEADS * HEAD_DIM
    nkw = SWA_KV_HEADS * HEAD_DIM
    qperm = np.arange(nqw).reshape(SWA_KV_HEADS, SWA_GROUP, HEAD_DIM).transpose(1, 0, 2).reshape(-1)
    qkv_perm = np.concatenate([qperm, np.arange(nqw, nqw + 2 * nkw)])
    sink_rows = np.repeat(np.arange(SWA_Q_HEADS), SUBLANES)

    outs = {name: [] for name in ("swa_kp", "swa_vp", "swa_ks", "swa_vs", "fox_kp", "fox_vp", "fox_lp",
                                  "fox_ks", "fox_vs", "fox_ls", "conv_p", "conv_s")}
    for i in range(depth):
        j, kind = i // N_MIXERS, i % N_MIXERS
        g_mix = row(norm_mix[i])
        if kind == 0:
            w = swa_w_qkv[j][:, qkv_perm].astype(BF16)
            b = row(swa_b_qkv[j][qkv_perm])
            wo = swa_w_o[j][qperm, :].astype(BF16)
            bo = row(swa_b_o[j])
            qp, kp, vp = _swa_proj(hp, g_mix, w, b, cos_p, sin_p, tm=tm_p, q_dtype=BF16)
            op = _swa_prompt_attn(qp, kp, vp, swa_sinks[j], batch=batch, seq=seq)
            qs, ks, vs = _swa_proj(hs, g_mix, w, b, cos_s, sin_s, tm=tm_s, q_dtype=F32)
            pad_rows = lambda a, n: jnp.pad(a.reshape(db, n_new, -1), ((0, 0), (0, n - n_new), (0, 0)))
            ck = cache_swa_k[j].reshape(db, WINDOW, nkw)
            cv = cache_swa_v[j].reshape(db, WINDOW, nkw)
            o8 = _swa_sample_attn(pad_rows(qs, SUBLANES), pad_rows(ks, BF16_ROWS), pad_rows(vs, BF16_ROWS), ck, cv,
                                  swa_sinks[j][sink_rows].reshape(-1, 1), n_new=n_new, bb=min(16, db))
            os_ = o8[:, :n_new].reshape(ts, nqw)
            kp4 = kp.reshape(batch, seq, SWA_KV_HEADS, HEAD_DIM)
            vp4 = vp.reshape(batch, seq, SWA_KV_HEADS, HEAD_DIM)
            outs["swa_kp"].append(kp4[:, seq - WINDOW:])
            outs["swa_vp"].append(vp4[:, seq - WINDOW:])
            new_k = ks.reshape(db, n_new, SWA_KV_HEADS, HEAD_DIM)
            new_v = vs.reshape(db, n_new, SWA_KV_HEADS, HEAD_DIM)
            outs["swa_ks"].append(jnp.concatenate([cache_swa_k[j][:, n_new:], new_k], axis=1))
            outs["swa_vs"].append(jnp.concatenate([cache_swa_v[j][:, n_new:], new_v], axis=1))
        elif kind == 1:
            w = fox_w_qkv[j].astype(BF16)
            wf = jnp.tile(fox_w_f[j], (1, LANES // FOX_HEADS)).astype(BF16)
            bf = row(jnp.tile(fox_b_f[j], LANES // FOX_HEADS))
            wo, bo = fox_w_o[j].astype(BF16), None
            kp, vp, lp, qa, ka, vt = _fox_proj_prompt(hp, g_mix, w, wf, bf, tm=min(256, seq), seq=seq)
            op = _fox_flash(qa, ka, vt, batch=batch, seq=seq, tq=min(1024, seq), tk=min(512, seq))
            outs["fox_kp"].append(kp.reshape(batch, FOX_HEADS, HEAD_DIM, seq).transpose(0, 3, 1, 2))
            outs["fox_vp"].append(vp.reshape(batch, FOX_HEADS, HEAD_DIM, seq).transpose(0, 3, 1, 2))
            outs["fox_lp"].append(lp.transpose(0, 2, 1))

            n_pool = cache_fox_k.shape[1]
            qs, ks, vs, ls = _fox_proj_sample(hs, g_mix, w, wf, bf)
            by_head = lambda a, rows: jnp.pad(a.reshape(db, n_new, FOX_HEADS, HEAD_DIM).transpose(0, 2, 1, 3),
                                              ((0, 0), (0, 0), (0, rows - n_new), (0, 0)))
            lfn = jnp.pad(ls.reshape(db, n_new, FOX_HEADS), ((0, 0), (0, BF16_ROWS - n_new), (0, 0)))
            o4 = _fox_sample_attn(cache_fox_k.transpose(0, 1, 3, 4, 2), cache_fox_v.transpose(0, 1, 3, 4, 2),
                                  cache_fox_logf.transpose(0, 1, 3, 2), page_table, by_head(qs, SUBLANES),
                                  by_head(ks, BF16_ROWS), by_head(vs, BF16_ROWS), lfn, layer=j, n_new=n_new)
            os_ = o4[:, :, :n_new].transpose(0, 2, 1, 3).reshape(ts, d)
            outs["fox_ks"].append(ks.reshape(db, n_new, FOX_HEADS, HEAD_DIM))
            outs["fox_vs"].append(vs.reshape(db, n_new, FOX_HEADS, HEAD_DIM))
            outs["fox_ls"].append(ls.reshape(db, n_new, FOX_HEADS))
        else:
            w = conv_w_in[j].astype(BF16)
            wo, bo = conv_w_out[j].astype(BF16), None
            op, up = _conv_proj(hp, g_mix, w, conv_w[j], None, tm=tm_p, seq=seq)
            st = state_conv[j]
            keep = CONV_WIDTH - 1
            s1 = jnp.pad(st[:, keep - 1:], ((0, 0), (0, n_new - 1), (0, 0))).reshape(ts, d)
            s2 = jnp.pad(st, ((0, 0), (0, n_new - keep), (0, 0))).reshape(ts, d)
            os_, us = _conv_proj(hs, g_mix, w, conv_w[j], (s1, s2), tm=tm_s, seq=n_new)
            outs["conv_p"].append(up[:, SUBLANES - keep:])
            outs["conv_s"].append(us.reshape(db, n_new, d)[:, n_new - keep:])

        wg, wu, wd = ffn_w_gate[i].astype(BF16), ffn_w_up[i].astype(BF16), ffn_w_down[i].astype(BF16)
        wpg, wpp = ple_w_gate[i].astype(BF16), ple_w_proj[i].astype(BF16)
        gn = row(norm_final) if i == depth - 1 else None
        common = dict(ffn_chunk=_ffn_chunk(wg.shape[1]))
        hp = _post(hp, op, wo, bo, row(norm_ffn[i]), wg, wu, wd, row(norm_ple[i]), wpg,
                   p_prompt[i].reshape(tp, -1), wpp, gn, tm=tm_p, **common)
        hs = _post(hs, os_, wo, bo, row(norm_ffn[i]), wg, wu, wd, row(norm_ple[i]), wpg,
                   p_sample[i].reshape(ts, -1), wpp, gn, tm=tm_s, **common)

    st = lambda name: jnp.stack(outs[name])
    return (hp.reshape(batch, seq, d), hs.reshape(db, n_new, d),
            st("swa_kp"), st("swa_vp"), st("swa_ks"), st("swa_vs"),
            st("fox_kp"), st("fox_vp"), st("fox_lp"),
            st("fox_ks"), st("fox_vs"), st("fox_ls"),
            st("conv_p"), st("conv_s"))
```

```python
import functools

import numpy as np
import jax
import jax.numpy as jnp
from jax import lax
from jax.experimental import pallas as pl
from jax.experimental.pallas import tpu as pltpu

F32 = jnp.float32
BF16 = jnp.bfloat16

HEAD_DIM = 64
HALF = HEAD_DIM // 2
SWA_Q_HEADS = 16
SWA_KV_HEADS = 4
SWA_GROUP = SWA_Q_HEADS // SWA_KV_HEADS
WINDOW = 128
FOX_HEADS = 16
PAGE_SIZE = 128
PAST_LEN = 8192
N_MIXERS = 3
CONV_WIDTH = 3
ROPE_THETA = 10000.0
EPS = 1e-6
NEG = -1e30
SCALE = HEAD_DIM ** -0.5

LANES = 128
SUBLANES = 8
BF16_ROWS = 16
VMEM_LIMIT = 56 * 1024 * 1024
AUG_COL = HEAD_DIM
FOX_PAGES_PER_STEP = 16
LOG2E = 1.4426950408889634


def _params(*sem):
    return pltpu.CompilerParams(dimension_semantics=sem, vmem_limit_bytes=VMEM_LIMIT)


def _const_spec(shape):
    nd = len(shape)
    return pl.BlockSpec(shape, lambda *_: (0,) * nd, pipeline_mode=pl.Buffered(1))


def _rms(x, g):
    ms = jnp.mean(x * x, axis=-1, keepdims=True)
    return x * lax.rsqrt(ms + EPS) * g


def _dot(a, b):
    return jnp.dot(a, b, preferred_element_type=F32)


def _dot_nt(a, b):
    return lax.dot_general(a, b, (((1,), (1,)), ((), ())), preferred_element_type=F32)


def _dot_exact(a, b):
    return jnp.dot(a, b, preferred_element_type=F32, precision=lax.Precision.HIGHEST)


def _sigmoid(x):
    return 1.0 / (1.0 + jnp.exp(-x))


def _log_sigmoid(z):
    return jnp.minimum(z, 0.0) - jnp.log1p(jnp.exp(-jnp.abs(z)))


def _swa_proj_kernel(x_ref, g_ref, w_ref, b_ref, cos_ref, sin_ref, q_ref, k_ref, v_ref, vt_ref):
    xn = _rms(x_ref[...], g_ref[...]).astype(BF16)
    qkv = _dot(xn, w_ref[...]) + b_ref[...]
    nq, nk = q_ref.shape[1], k_ref.shape[1]
    cos, sin = cos_ref[...], sin_ref[...]
    lane = lax.broadcasted_iota(jnp.int32, cos.shape, 1)
    first_half = (lane % HEAD_DIM) < HALF

    def rope(xs):
        sw = jnp.where(first_half, pltpu.roll(xs, LANES - HALF, 1), pltpu.roll(xs, HALF, 1))
        return xs * cos + sw * sin

    for j in range(nq // LANES):
        q_ref[:, j * LANES:(j + 1) * LANES] = rope(qkv[:, j * LANES:(j + 1) * LANES]).astype(q_ref.dtype)
    for j in range(nk // LANES):
        k_ref[:, j * LANES:(j + 1) * LANES] = rope(qkv[:, nq + j * LANES:nq + (j + 1) * LANES])
    v_ref[...] = qkv[:, nq + nk:]
    vt_ref[...] = qkv[:, nq + nk:].T.astype(BF16)


def _swa_proj(x, g, w, b, cos_t, sin_t, *, tm, q_dtype):
    t, d = x.shape
    nq = SWA_Q_HEADS * HEAD_DIM
    nk = SWA_KV_HEADS * HEAD_DIM
    n_pos_blocks = cos_t.shape[0] // tm
    return pl.pallas_call(
        _swa_proj_kernel,
        grid=(t // tm,),
        in_specs=[
            pl.BlockSpec((tm, d), lambda i: (i, 0)),
            _const_spec((1, d)),
            _const_spec(w.shape),
            _const_spec((1, w.shape[1])),
            pl.BlockSpec((tm, LANES), lambda i: (i % n_pos_blocks, 0)),
            pl.BlockSpec((tm, LANES), lambda i: (i % n_pos_blocks, 0)),
        ],
        out_specs=[
            pl.BlockSpec((tm, nq), lambda i: (i, 0)),
            pl.BlockSpec((tm, nk), lambda i: (i, 0)),
            pl.BlockSpec((tm, nk), lambda i: (i, 0)),
            pl.BlockSpec((nk, tm), lambda i: (0, i)),
        ],
        out_shape=[
            jax.ShapeDtypeStruct((t, nq), q_dtype),
            jax.ShapeDtypeStruct((t, nk), F32),
            jax.ShapeDtypeStruct((t, nk), F32),
            jax.ShapeDtypeStruct((nk, t), BF16),
        ],
        compiler_params=_params("parallel"),
        name="swa_proj",
    )(x, g, w, b, cos_t, sin_t)


def _swa_prompt_kernel(sink_ref, q_ref, kp_ref, kc_ref, vtp_ref, vtc_ref, o_ref):
    i = pl.program_id(1)
    w = WINDOW
    kband = jnp.concatenate([kp_ref[...], kc_ref[...]], axis=0)
    vtband = jnp.concatenate([vtp_ref[...], vtc_ref[...]], axis=1)
    kj = lax.broadcasted_iota(jnp.int32, (2 * w, w), 0)
    qi = lax.broadcasted_iota(jnp.int32, (2 * w, w), 1)
    dist = qi + w - kj
    mask = (dist >= 0) & (dist < w) & ((kj >= w) | (i > 0))
    low = lax.broadcasted_iota(jnp.int32, (2 * w, LANES), 1) < HEAD_DIM
    kz = []
    for kvp in range(SWA_KV_HEADS // 2):
        kt = kband[:, kvp * LANES:(kvp + 1) * LANES]
        kz += [jnp.where(low, kt, 0.0).astype(BF16), jnp.where(low, 0.0, kt).astype(BF16)]
    pieces = []
    for g in range(SWA_GROUP):
        for kv in range(SWA_KV_HEADS):
            c0 = g * SWA_KV_HEADS * HEAD_DIM + (kv // 2) * LANES
            st = _dot_nt(kz[kv], q_ref[:, c0:c0 + LANES]) * SCALE
            pieces.append(jnp.where(mask, st, NEG))
    s = jnp.concatenate(pieces, axis=1)
    sk = sink_ref[...]
    m = jnp.maximum(jnp.max(s, axis=0, keepdims=True), sk)
    ex = jnp.exp(s - m)
    den = jnp.sum(ex, axis=0, keepdims=True) + jnp.exp(sk - m)
    p = ex.astype(BF16)
    outs = []
    for n in range(SWA_Q_HEADS):
        kv = n % SWA_KV_HEADS
        ot = _dot(vtband[kv * HEAD_DIM:(kv + 1) * HEAD_DIM, :], p[:, n * w:(n + 1) * w])
        outs.append(ot / den[:, n * w:(n + 1) * w])
    o_ref[...] = jnp.concatenate(outs, axis=0).T.astype(o_ref.dtype)


def _swa_prompt_attn(q, k, vt, sink_row, *, batch, seq):
    nb = seq // WINDOW
    nq, nk = q.shape[1], k.shape[1]
    cur = lambda b, i: (b * nb + i, 0)
    prev = lambda b, i: (b * nb + jnp.maximum(i - 1, 0), 0)
    cur_t = lambda b, i: (0, b * nb + i)
    prev_t = lambda b, i: (0, b * nb + jnp.maximum(i - 1, 0))
    return pl.pallas_call(
        _swa_prompt_kernel,
        grid=(batch, nb),
        in_specs=[
            _const_spec(sink_row.shape),
            pl.BlockSpec((WINDOW, nq), cur),
            pl.BlockSpec((WINDOW, nk), prev),
            pl.BlockSpec((WINDOW, nk), cur),
            pl.BlockSpec((nk, WINDOW), prev_t),
            pl.BlockSpec((nk, WINDOW), cur_t),
        ],
        out_specs=pl.BlockSpec((WINDOW, nq), cur),
        out_shape=jax.ShapeDtypeStruct(q.shape, BF16),
        compiler_params=_params("parallel", "parallel"),
        name="swa_prompt_attn",
    )(sink_row, q, k, k, vt, vt)


def _swa_sample_kernel(sink_ref, q_ref, kn_ref, vn_ref, ck_ref, cv_ref, o_ref, *, n_new):
    bb = q_ref.shape[0]
    buf = ck_ref.shape[4]
    rows = SWA_Q_HEADS * SUBLANES
    kvw = SWA_KV_HEADS * HEAD_DIM
    lane = lax.broadcasted_iota(jnp.int32, (SUBLANES, kvw), 1)
    head_mask = [(lane // HEAD_DIM) == kv for kv in range(SWA_KV_HEADS)]
    t = lax.broadcasted_iota(jnp.int32, (rows, 1), 0) % SUBLANES
    valid_c = lax.broadcasted_iota(jnp.int32, (rows, buf), 1) > t
    tn = lax.broadcasted_iota(jnp.int32, (rows, BF16_ROWS), 1)
    valid_n = (tn <= t) & (tn < n_new)
    sk = sink_ref[...]

    def body(b, carry):
        q8 = q_ref[b]
        pieces = []
        for kv in range(SWA_KV_HEADS):
            for g in range(SWA_GROUP):
                pieces.append(jnp.where(head_mask[kv], q8[:, g * kvw:(g + 1) * kvw], 0.0))
        qbd = jnp.concatenate(pieces, axis=0).astype(BF16)
        ckt = ck_ref[0, b].reshape(kvw, buf).astype(BF16)
        cvt = cv_ref[0, b].reshape(kvw, buf).astype(BF16)
        s_c = jnp.where(valid_c, _dot(qbd, ckt) * SCALE, NEG)
        s_n = jnp.where(valid_n, _dot_nt(qbd, kn_ref[b].astype(BF16)) * SCALE, NEG)
        m = jnp.maximum(jnp.maximum(jnp.max(s_c, axis=-1, keepdims=True),
                                    jnp.max(s_n, axis=-1, keepdims=True)), sk)
        e_c = jnp.exp(s_c - m)
        e_n = jnp.exp(s_n - m)
        den = jnp.sum(e_c, axis=-1, keepdims=True) + jnp.sum(e_n, axis=-1, keepdims=True) + jnp.exp(sk - m)
        o = (_dot_nt((e_c / den).astype(BF16), cvt)
             + _dot((e_n / den).astype(BF16), vn_ref[b].astype(BF16)))
        for g in range(SWA_GROUP):
            acc = None
            for kv in range(SWA_KV_HEADS):
                r0 = (kv * SWA_GROUP + g) * SUBLANES
                part = jnp.where(head_mask[kv], o[r0:r0 + SUBLANES], 0.0)
                acc = part if acc is None else acc + part
            o_ref[b, :, g * kvw:(g + 1) * kvw] = acc
        return carry

    lax.fori_loop(0, bb, body, 0)


def _swa_sample_attn(q8, kn16, vn16, cache_kt, cache_vt, sink_col, *, layer, n_new, bb):
    db = cache_kt.shape[1]
    kvw = cache_kt.shape[2] * cache_kt.shape[3]
    nq = q8.shape[2]
    cache_spec = pl.BlockSpec((1, bb) + cache_kt.shape[2:], lambda i: (layer, i, 0, 0, 0))
    return pl.pallas_call(
        functools.partial(_swa_sample_kernel, n_new=n_new),
        grid=(db // bb,),
        in_specs=[
            _const_spec(sink_col.shape),
            pl.BlockSpec((bb, SUBLANES, nq), lambda i: (i, 0, 0)),
            pl.BlockSpec((bb, BF16_ROWS, kvw), lambda i: (i, 0, 0)),
            pl.BlockSpec((bb, BF16_ROWS, kvw), lambda i: (i, 0, 0)),
            cache_spec,
            cache_spec,
        ],
        out_specs=pl.BlockSpec((bb, SUBLANES, nq), lambda i: (i, 0, 0)),
        out_shape=jax.ShapeDtypeStruct(q8.shape, F32),
        compiler_params=_params("parallel"),
        name="swa_sample_attn",
    )(sink_col, q8, kn16, vn16, cache_kt, cache_vt)


def _fox_logf(xn, wf_ref, bf_ref):
    return _log_sigmoid(_dot(xn, wf_ref[...]) + bf_ref[...])


def _fox_proj_prompt_kernel(x_ref, g_ref, w_ref, wf_ref, bf_ref, pq_ref, pk_ref, oq_ref, ok_ref,
                            k_ref, v_ref, lf_ref, qa_ref, ka_ref, vt_ref, carry_ref, *, tiles_per_seq):
    i = pl.program_id(0)
    tm, d = x_ref.shape

    @pl.when(i % tiles_per_seq == 0)
    def _():
        carry_ref[...] = jnp.zeros_like(carry_ref)

    xn = _rms(x_ref[...], g_ref[...]).astype(BF16)
    qkv = _dot(xn, w_ref[...])
    vt = qkv[:, 2 * d:].T
    k_ref[0] = qkv[:, d:2 * d].T
    v_ref[0] = vt
    vt_ref[...] = vt.astype(BF16)
    logf = _fox_logf(xn, wf_ref, bf_ref)
    lf_ref[0] = logf.T[:FOX_HEADS, :]

    r = lax.broadcasted_iota(jnp.int32, (tm, tm), 0)
    c = lax.broadcasted_iota(jnp.int32, (tm, tm), 1)
    ltri = jnp.where(c <= r, 1.0, 0.0).astype(F32)
    dcum = _dot_exact(ltri, logf) + carry_ref[...]
    carry_ref[...] = dcum[tm - 1:tm, :]

    d2 = dcum * LOG2E
    hi = d2.astype(BF16).astype(F32)
    r1 = d2 - hi
    mid = r1.astype(BF16).astype(F32)
    lo = r1 - mid
    grp = (lax.broadcasted_iota(jnp.int32, (tm, LANES), 1) // FOX_HEADS) % 3
    split = jnp.where(grp == 0, hi, jnp.where(grp == 1, mid, lo)).astype(BF16)
    aug_q = _dot(split, pq_ref[...]) + oq_ref[...]
    aug_k = _dot(split, pk_ref[...]) + ok_ref[...]

    low = lax.broadcasted_iota(jnp.int32, (tm, LANES), 1) < HEAD_DIM
    for h in range(FOX_HEADS):
        p, e = h // 2, h % 2
        qp = qkv[:, p * LANES:(p + 1) * LANES] * (SCALE * LOG2E)
        kp = qkv[:, d + p * LANES:d + (p + 1) * LANES]
        if e == 1:
            qp = pltpu.roll(qp, HEAD_DIM, 1)
            kp = pltpu.roll(kp, HEAD_DIM, 1)
        qa_ref[:, h * LANES:(h + 1) * LANES] = jnp.where(low, qp, aug_q[:, h * LANES:(h + 1) * LANES]).astype(BF16)
        ka_ref[:, h * LANES:(h + 1) * LANES] = jnp.where(low, kp, aug_k[:, h * LANES:(h + 1) * LANES]).astype(BF16)


def _fox_aug_constants():
    pq = np.zeros((LANES, FOX_HEADS * LANES), np.float32)
    pk = np.zeros((LANES, FOX_HEADS * LANES), np.float32)
    oq = np.zeros((1, FOX_HEADS * LANES), np.float32)
    ok = np.zeros((1, FOX_HEADS * LANES), np.float32)
    for h in range(FOX_HEADS):
        base = h * LANES + AUG_COL
        for c in range(3):
            pk[c * FOX_HEADS + h, base + c] = -1.0
            oq[0, base + c] = 1.0
            pq[c * FOX_HEADS + h, base + 3 + c] = 1.0
            ok[0, base + 3 + c] = 1.0
    return jnp.asarray(pq, BF16), jnp.asarray(pk, BF16), jnp.asarray(oq), jnp.asarray(ok)


def _fox_proj_prompt(x, g, w, wf, bf, *, tm, seq):
    t, d = x.shape
    pq, pk, oq, ok = _fox_aug_constants()
    aw = FOX_HEADS * LANES
    tiles_per_seq = seq // tm
    row = lambda i: (i, 0)
    by_seq = lambda i: (i // tiles_per_seq, 0, i % tiles_per_seq)
    return pl.pallas_call(
        functools.partial(_fox_proj_prompt_kernel, tiles_per_seq=tiles_per_seq),
        grid=(t // tm,),
        in_specs=[
            pl.BlockSpec((tm, d), row),
            _const_spec((1, d)),
            _const_spec(w.shape),
            _const_spec(wf.shape),
            _const_spec(bf.shape),
            _const_spec(pq.shape),
            _const_spec(pk.shape),
            _const_spec(oq.shape),
            _const_spec(ok.shape),
        ],
        out_specs=[
            pl.BlockSpec((1, d, tm), by_seq),
            pl.BlockSpec((1, d, tm), by_seq),
            pl.BlockSpec((1, FOX_HEADS, tm), by_seq),
            pl.BlockSpec((tm, aw), row),
            pl.BlockSpec((tm, aw), row),
            pl.BlockSpec((d, tm), lambda i: (0, i)),
        ],
        out_shape=[
            jax.ShapeDtypeStruct((t // seq, d, seq), F32),
            jax.ShapeDtypeStruct((t // seq, d, seq), F32),
            jax.ShapeDtypeStruct((t // seq, FOX_HEADS, seq), F32),
            jax.ShapeDtypeStruct((t, aw), BF16),
            jax.ShapeDtypeStruct((t, aw), BF16),
            jax.ShapeDtypeStruct((d, t), BF16),
        ],
        scratch_shapes=[pltpu.VMEM((1, LANES), F32)],
        compiler_params=_params("arbitrary"),
        name="fox_proj_prompt",
    )(x, g, w, wf, bf, pq, pk, oq, ok)


def _fox_proj_sample_kernel(x_ref, g_ref, w_ref, wf_ref, bf_ref, q_ref, k_ref, v_ref, lf_ref):
    d = k_ref.shape[1]
    xn = _rms(x_ref[...], g_ref[...]).astype(BF16)
    qkv = _dot(xn, w_ref[...])
    q_ref[...] = qkv[:, :d]
    k_ref[...] = qkv[:, d:2 * d]
    v_ref[...] = qkv[:, 2 * d:]
    lf_ref[...] = _fox_logf(xn, wf_ref, bf_ref)[:, :FOX_HEADS]


def _fox_proj_sample(x, g, w, wf, bf):
    t, d = x.shape
    return pl.pallas_call(
        _fox_proj_sample_kernel,
        grid=(1,),
        in_specs=[
            _const_spec((t, d)),
            _const_spec((1, d)),
            _const_spec(w.shape),
            _const_spec(wf.shape),
            _const_spec(bf.shape),
        ],
        out_specs=[
            _const_spec((t, d)), _const_spec((t, d)), _const_spec((t, d)),
            _const_spec((t, FOX_HEADS)),
        ],
        out_shape=[
            jax.ShapeDtypeStruct((t, d), F32),
            jax.ShapeDtypeStruct((t, d), F32),
            jax.ShapeDtypeStruct((t, d), F32),
            jax.ShapeDtypeStruct((t, FOX_HEADS), F32),
        ],
        compiler_params=_params("arbitrary"),
        name="fox_proj_sample",
    )(x, g, w, wf, bf)


def _fox_flash_kernel(qi_ref, kj_ref, q_ref, k_ref, vt_ref, o_ref, m_ref, l_ref, acc_ref):
    step = pl.program_id(2)
    i, j = qi_ref[step], kj_ref[step]
    tq, tk = q_ref.shape[0], k_ref.shape[0]

    @pl.when(j == 0)
    def _():
        m_ref[...] = jnp.full_like(m_ref, NEG)
        l_ref[...] = jnp.zeros_like(l_ref)
        acc_ref[...] = jnp.zeros_like(acc_ref)

    def update(masked):
        for e in range(2):
            st = _dot_nt(k_ref[:, e * LANES:(e + 1) * LANES], q_ref[:, e * LANES:(e + 1) * LANES])
            if masked:
                key = j * tk + lax.broadcasted_iota(jnp.int32, (tk, tq), 0)
                qry = i * tq + lax.broadcasted_iota(jnp.int32, (tk, tq), 1)
                st = jnp.where(key > qry, NEG, st)
            m_prev = m_ref[e]
            m_new = jnp.maximum(m_prev, jnp.max(st, axis=0, keepdims=True))
            alpha = jnp.exp2(m_prev - m_new)
            p = jnp.exp2(st - m_new)
            l_ref[e] = alpha * l_ref[e] + jnp.sum(p, axis=0, keepdims=True)
            m_ref[e] = m_new
            rows = slice(e * HEAD_DIM, (e + 1) * HEAD_DIM)
            acc_ref[rows, :] = acc_ref[rows, :] * alpha + _dot(vt_ref[rows, :], p.astype(BF16))

    crosses_diagonal = (j + 1) * tk - 1 > i * tq
    pl.when(crosses_diagonal)(functools.partial(update, True))
    pl.when(jnp.logical_not(crosses_diagonal))(functools.partial(update, False))

    @pl.when(j == ((i + 1) * tq - 1) // tk)
    def _():
        ot = jnp.concatenate([acc_ref[e * HEAD_DIM:(e + 1) * HEAD_DIM, :] / l_ref[e] for e in range(2)], axis=0)
        o_ref[...] = ot.T.astype(o_ref.dtype)


def _fox_flash(qa, ka, vt, *, batch, seq, tq, tk):
    nq, nk = seq // tq, seq // tk
    pairs = [(i, j) for i in range(nq) for j in range(((i + 1) * tq - 1) // tk + 1)]
    qi = jnp.asarray([p[0] for p in pairs], jnp.int32)
    kj = jnp.asarray([p[1] for p in pairs], jnp.int32)
    d, t = vt.shape
    grid_spec = pltpu.PrefetchScalarGridSpec(
        num_scalar_prefetch=2,
        grid=(batch, FOX_HEADS // 2, len(pairs)),
        in_specs=[
            pl.BlockSpec((tq, 2 * LANES), lambda b, p, s, qi, kj: (b * nq + qi[s], p)),
            pl.BlockSpec((tk, 2 * LANES), lambda b, p, s, qi, kj: (b * nk + kj[s], p)),
            pl.BlockSpec((LANES, tk), lambda b, p, s, qi, kj: (p, b * nk + kj[s])),
        ],
        out_specs=pl.BlockSpec((tq, LANES), lambda b, p, s, qi, kj: (b * nq + qi[s], p)),
        scratch_shapes=[
            pltpu.VMEM((2, 1, tq), F32),
            pltpu.VMEM((2, 1, tq), F32),
            pltpu.VMEM((LANES, tq), F32),
        ],
    )
    return pl.pallas_call(
        _fox_flash_kernel,
        grid_spec=grid_spec,
        out_shape=jax.ShapeDtypeStruct((t, d), BF16),
        compiler_params=_params("parallel", "parallel", "arbitrary"),
        name="fox_flash",
    )(qi, kj, qa, ka, vt)


def _fox_sample_kernel(pt_ref, *refs, n_new):
    n = FOX_PAGES_PER_STEP
    k_refs, v_refs, lf_refs = refs[:n], refs[n:2 * n], refs[2 * n:3 * n]
    q_ref, kn_ref, vn_ref, lfn_ref, o_ref, m_ref, l_ref, acc_ref, carry_ref = refs[3 * n:]
    j = pl.program_id(1)

    @pl.when(j == 0)
    def _():
        m_ref[...] = jnp.full_like(m_ref, NEG)
        l_ref[...] = jnp.zeros_like(l_ref)
        acc_ref[...] = jnp.zeros_like(acc_ref)
        carry_ref[...] = jnp.zeros_like(carry_ref)

    ui = lax.broadcasted_iota(jnp.int32, (PAGE_SIZE, PAGE_SIZE), 0)
    uj = lax.broadcasted_iota(jnp.int32, (PAGE_SIZE, PAGE_SIZE), 1)
    upper = jnp.where(ui <= uj, 1.0, 0.0).astype(F32)
    within = _dot_exact(jnp.concatenate([r[0, 0] for r in lf_refs], axis=0), upper)
    carry = carry_ref[...]
    parts = []
    for pg in range(n):
        dpg = within[pg * FOX_HEADS:(pg + 1) * FOX_HEADS] + carry
        carry = dpg[:, PAGE_SIZE - 1:PAGE_SIZE]
        parts.append(dpg)
    carry_ref[...] = carry
    dpast = jnp.concatenate(parts, axis=1)

    def keys_on_lanes(page_refs, h):
        return jnp.concatenate([r[0, 0, h] for r in page_refs], axis=1).astype(BF16)

    heads = range(FOX_HEADS)
    q = [(q_ref[0, h] * SCALE).astype(BF16) for h in heads]

    def per_head_rows(x):
        return jnp.concatenate([jnp.broadcast_to(x[h:h + 1, :], (SUBLANES, x.shape[1])) for h in heads], axis=0)

    def head_block(x, h):
        return x[h * SUBLANES:(h + 1) * SUBLANES]

    s = jnp.concatenate([_dot(q[h], keys_on_lanes(k_refs, h)) for h in heads], axis=0) - per_head_rows(dpast)
    m_prev = m_ref[...]
    m_new = jnp.maximum(m_prev, jnp.max(s, axis=-1, keepdims=True))
    alpha = jnp.exp(m_prev - m_new)
    p = jnp.exp(s - m_new)
    l_ref[...] = alpha * l_ref[...] + jnp.sum(p, axis=-1, keepdims=True)
    m_ref[...] = m_new
    pv = jnp.concatenate([_dot_nt(head_block(p, h).astype(BF16), keys_on_lanes(v_refs, h)) for h in heads], axis=0)
    acc_ref[...] = acc_ref[...] * alpha + pv

    @pl.when(j == pl.num_programs(1) - 1)
    def _():
        rows = lfn_ref.shape[1]
        ti = lax.broadcasted_iota(jnp.int32, (rows, rows), 0)
        tj = lax.broadcasted_iota(jnp.int32, (rows, rows), 1)
        last_key = jnp.where(lax.broadcasted_iota(jnp.int32, (rows, PAGE_SIZE), 1) == PAGE_SIZE - 1, 1.0, 0.0)
        d_last = lax.dot_general(last_key.astype(F32), parts[-1], (((1,), (1,)), ((), ())),
                                 precision=lax.Precision.HIGHEST, preferred_element_type=F32)
        d_new = d_last + _dot_exact(jnp.where(tj <= ti, 1.0, 0.0).astype(F32), lfn_ref[0])
        d_new_t = lax.dot_general(jnp.where(ti == tj, 1.0, 0.0).astype(F32), d_new, (((1,), (1,)), ((), ())),
                                  precision=lax.Precision.HIGHEST, preferred_element_type=F32)
        dq = jnp.concatenate([d_new[:SUBLANES, h:h + 1] for h in heads], axis=0)
        s_n = (jnp.concatenate([_dot_nt(q[h], kn_ref[0, h].astype(BF16)) for h in heads], axis=0)
               - per_head_rows(d_new_t) + dq)
        t_q = lax.broadcasted_iota(jnp.int32, s_n.shape, 0) % SUBLANES
        t_k = lax.broadcasted_iota(jnp.int32, s_n.shape, 1)
        s_n = jnp.where((t_k <= t_q) & (t_k < n_new), s_n, NEG)
        m_old = m_ref[...] + dq
        m_fin = jnp.maximum(m_old, jnp.max(s_n, axis=-1, keepdims=True))
        a_fin = jnp.exp(m_old - m_fin)
        p_n = jnp.exp(s_n - m_fin)
        l_fin = a_fin * l_ref[...] + jnp.sum(p_n, axis=-1, keepdims=True)
        pv_n = jnp.concatenate([_dot(head_block(p_n, h).astype(BF16), vn_ref[0, h].astype(BF16)) for h in heads],
                               axis=0)
        o = (acc_ref[...] * a_fin + pv_n) / l_fin
        for h in heads:
            o_ref[0, h] = head_block(o, h)


def _fox_sample_attn(kt_pages, vt_pages, lt_pages, page_table, q4, kn4, vn4, lfn, *, layer, n_new):
    db, n_pages = page_table.shape
    n = FOX_PAGES_PER_STEP
    steps = n_pages // n

    def pool_spec(k, arr):
        nd = arr.ndim - 2
        return pl.BlockSpec((1, 1) + arr.shape[2:], lambda b, j, pt: (layer, pt[b, j * n + k]) + (0,) * nd)

    def per_seq(arr):
        nd = arr.ndim - 1
        return pl.BlockSpec((1,) + arr.shape[1:], lambda b, j, pt: (b,) + (0,) * nd)

    grid_spec = pltpu.PrefetchScalarGridSpec(
        num_scalar_prefetch=1,
        grid=(db, steps),
        in_specs=([pool_spec(k, kt_pages) for k in range(n)] + [pool_spec(k, vt_pages) for k in range(n)]
                  + [pool_spec(k, lt_pages) for k in range(n)]
                  + [per_seq(q4), per_seq(kn4), per_seq(vn4), per_seq(lfn)]),
        out_specs=per_seq(q4),
        scratch_shapes=[
            pltpu.VMEM((FOX_HEADS * SUBLANES, 1), F32),
            pltpu.VMEM((FOX_HEADS * SUBLANES, 1), F32),
            pltpu.VMEM((FOX_HEADS * SUBLANES, HEAD_DIM), F32),
            pltpu.VMEM((FOX_HEADS, 1), F32),
        ],
    )
    return pl.pallas_call(
        functools.partial(_fox_sample_kernel, n_new=n_new),
        grid_spec=grid_spec,
        out_shape=jax.ShapeDtypeStruct(q4.shape, F32),
        compiler_params=_params("parallel", "arbitrary"),
        name="fox_sample_attn",
    )(page_table, *([kt_pages] * n), *([vt_pages] * n), *([lt_pages] * n), q4, kn4, vn4, lfn)


def _conv_proj_kernel(*refs, seq, has_state):
    if has_state:
        x_ref, g_ref, w_ref, cw_ref, s1_ref, s2_ref, y_ref, u_ref, carry_ref = refs
    else:
        x_ref, g_ref, w_ref, cw_ref, y_ref, u_ref, carry_ref = refs
    i = pl.program_id(0)
    tm, d = x_ref.shape

    @pl.when(i == 0)
    def _():
        carry_ref[...] = jnp.zeros_like(carry_ref)

    xn = _rms(x_ref[...], g_ref[...]).astype(BF16)
    proj = _dot(xn, w_ref[...])
    gb, gc, hh = proj[:, :d], proj[:, d:2 * d], proj[:, 2 * d:]
    u = gc * hh
    row = lax.broadcasted_iota(jnp.int32, (tm, 1), 0)
    pos = (i * tm + row) % seq
    prev_tile = carry_ref[...]
    um1 = jnp.where(row == 0, prev_tile[7:8, :], pltpu.roll(u, 1, 0))
    um2 = jnp.where(row == 0, prev_tile[6:7, :], jnp.where(row == 1, prev_tile[7:8, :], pltpu.roll(u, 2, 0)))
    um1 = jnp.where(pos >= 1, um1, 0.0)
    um2 = jnp.where(pos >= 2, um2, 0.0)
    if has_state:
        um1 = um1 + s1_ref[...]
        um2 = um2 + s2_ref[...]
    cw = cw_ref[...]
    conv = cw[0:1, :] * um2 + cw[1:2, :] * um1 + cw[2:3, :] * u
    y_ref[...] = (gb * conv).astype(y_ref.dtype)
    carry_ref[...] = u[tm - SUBLANES:, :]
    if has_state:
        u_ref[...] = u
    else:
        u_ref[0] = u[tm - SUBLANES:, :]


def _conv_proj(x, g, w, cw, state_rows, *, tm, seq):
    t, d = x.shape
    has_state = state_rows is not None
    row = lambda i: (i, 0)
    in_specs = [pl.BlockSpec((tm, d), row), _const_spec((1, d)), _const_spec(w.shape), _const_spec(cw.shape)]
    args = [x, g, w, cw]
    if has_state:
        in_specs += [pl.BlockSpec((tm, d), row), pl.BlockSpec((tm, d), row)]
        args += list(state_rows)
        u_spec = pl.BlockSpec((tm, d), row)
        u_shape = jax.ShapeDtypeStruct((t, d), F32)
    else:
        tiles_per_seq = seq // tm
        u_spec = pl.BlockSpec((1, SUBLANES, d), lambda i: (i // tiles_per_seq, 0, 0))
        u_shape = jax.ShapeDtypeStruct((t // seq, SUBLANES, d), F32)
    return pl.pallas_call(
        functools.partial(_conv_proj_kernel, seq=seq, has_state=has_state),
        grid=(t // tm,),
        in_specs=in_specs,
        out_specs=[pl.BlockSpec((tm, d), row), u_spec],
        out_shape=[jax.ShapeDtypeStruct((t, d), BF16), u_shape],
        scratch_shapes=[pltpu.VMEM((SUBLANES, d), F32)],
        compiler_params=_params("arbitrary"),
        name="conv_proj",
    )(*args)


def _post_kernel(*refs, has_bias, final_norm, ffn_chunk):
    refs = list(refs)
    h_ref, o_ref, wo_ref = refs[:3]
    refs = refs[3:]
    bo_ref = refs.pop(0) if has_bias else None
    gf_ref, wg_ref, wu_ref, wd_ref, gp_ref, wpg_ref, p_ref, wpp_ref = refs[:8]
    refs = refs[8:]
    gn_ref = refs.pop(0) if final_norm else None
    out_ref, act_ref = refs

    y = _dot(o_ref[...].astype(BF16), wo_ref[...])
    if has_bias:
        y = y + bo_ref[...]
    h1 = h_ref[...] + y
    hn = _rms(h1, gf_ref[...]).astype(BF16)
    ffn = wg_ref.shape[1]
    for c in range(ffn // ffn_chunk):
        sl = slice(c * ffn_chunk, (c + 1) * ffn_chunk)
        gate = _dot(hn, wg_ref[:, sl])
        up = _dot(hn, wu_ref[:, sl])
        act_ref[:, sl] = (gate * _sigmoid(gate) * up).astype(BF16)
    h2 = h1 + _dot(act_ref[...], wd_ref[...])
    pg = _sigmoid(_dot(_rms(h2, gp_ref[...]).astype(BF16), wpg_ref[...]))
    h3 = h2 + pg * _dot(p_ref[0].astype(BF16), wpp_ref[...])
    if final_norm:
        h3 = _rms(h3, gn_ref[...])
    out_ref[...] = h3


def _post(h, o, wo, bo, gf, wg, wu, wd, gp, wpg, p, wpp, gn, *, layer, tm, ffn_chunk):
    t, d = h.shape
    row = lambda i: (i, 0)
    in_specs = [pl.BlockSpec((tm, d), row), pl.BlockSpec((tm, o.shape[1]), row), _const_spec(wo.shape)]
    args = [h, o, wo]
    if bo is not None:
        in_specs.append(_const_spec(bo.shape))
        args.append(bo)
    in_specs += [_const_spec(gf.shape), _const_spec(wg.shape), _const_spec(wu.shape), _const_spec(wd.shape),
                 _const_spec(gp.shape), _const_spec(wpg.shape),
                 pl.BlockSpec((1, tm, p.shape[2]), lambda i: (layer, i, 0)),
                 _const_spec(wpp.shape)]
    args += [gf, wg, wu, wd, gp, wpg, p, wpp]
    if gn is not None:
        in_specs.append(_const_spec(gn.shape))
        args.append(gn)
    return pl.pallas_call(
        functools.partial(_post_kernel, has_bias=bo is not None, final_norm=gn is not None, ffn_chunk=ffn_chunk),
        grid=(t // tm,),
        in_specs=in_specs,
        out_specs=pl.BlockSpec((tm, d), row),
        out_shape=jax.ShapeDtypeStruct((t, d), F32),
        scratch_shapes=[pltpu.VMEM((tm, wg.shape[1]), BF16)],
        compiler_params=_params("parallel"),
        name="post_mixer",
    )(*args)


def _rope_tables(pos):
    inv = ROPE_THETA ** (-jnp.arange(HALF, dtype=F32) / HALF)
    ang = pos.astype(F32)[:, None] * inv[None, :]
    cos, sin = jnp.cos(ang), jnp.sin(ang)
    reps = LANES // HEAD_DIM
    return jnp.tile(cos, (1, 2 * reps)), jnp.tile(jnp.concatenate([-sin, sin], axis=1), (1, reps))


def _ffn_chunk(ffn):
    for c in (512, 256, 128):
        if ffn % c == 0:
            return c
    return ffn


def kernel(x_prompt, x_sample, cache_swa_k, cache_swa_v, cache_fox_k, cache_fox_v, cache_fox_logf, state_conv, page_table, p_prompt, p_sample, norm_mix, norm_ffn, norm_ple, norm_final, swa_w_qkv, swa_b_qkv, swa_w_o, swa_b_o, swa_sinks, fox_w_qkv, fox_w_f, fox_b_f, fox_w_o, conv_w_in, conv_w, conv_w_out, ffn_w_gate, ffn_w_up, ffn_w_down, ple_w_gate, ple_w_proj):
    batch, seq, d = x_prompt.shape
    db, n_new, _ = x_sample.shape
    depth = norm_mix.shape[0]
    tp, ts = batch * seq, db * n_new
    assert d == FOX_HEADS * HEAD_DIM == SWA_Q_HEADS * HEAD_DIM
    assert cache_swa_k.shape[2] == WINDOW and seq % WINDOW == 0 and n_new <= SUBLANES
    tm_p = min(512, seq)
    tm_s = ts

    hp = x_prompt.reshape(tp, d)
    hs = x_sample.reshape(ts, d)
    row = lambda a: a.reshape(1, -1)

    cos_p, sin_p = _rope_tables(jnp.arange(seq))
    cos_s, sin_s = _rope_tables(jnp.tile(PAST_LEN + jnp.arange(n_new), db))

    nqw = SWA_Q_HEADS * HEAD_DIM
    nkw = SWA_KV_HEADS * HEAD_DIM
    qperm = np.arange(nqw).reshape(SWA_KV_HEADS, SWA_GROUP, HEAD_DIM).transpose(1, 0, 2).reshape(-1)
    qkv_perm = np.concatenate([qperm, np.arange(nqw, nqw + 2 * nkw)])
    sink_rows = np.repeat(np.arange(SWA_Q_HEADS), SUBLANES)
    sink_heads = np.arange(SWA_Q_HEADS).reshape(SWA_KV_HEADS, SWA_GROUP).T.reshape(-1)

    outs = {name: [] for name in ("swa_kp", "swa_vp", "swa_ks", "swa_vs", "fox_kp", "fox_vp", "fox_lp",
                                  "fox_ks", "fox_vs", "fox_ls", "conv_p", "conv_s")}
    for i in range(depth):
        j, kind = i // N_MIXERS, i % N_MIXERS
        g_mix = row(norm_mix[i])
        if kind == 0:
            w = swa_w_qkv[j][:, qkv_perm].astype(BF16)
            b = row(swa_b_qkv[j][qkv_perm])
            wo = swa_w_o[j][qperm, :].astype(BF16)
            bo = row(swa_b_o[j])
            qp, kp, vp, vtp = _swa_proj(hp, g_mix, w, b, cos_p, sin_p, tm=tm_p, q_dtype=BF16)
            sink_row = jnp.repeat(swa_sinks[j][sink_heads], WINDOW).reshape(1, -1)
            op = _swa_prompt_attn(qp, kp, vtp, sink_row, batch=batch, seq=seq)
            qs, ks, vs, _ = _swa_proj(hs, g_mix, w, b, cos_s, sin_s, tm=tm_s, q_dtype=F32)
            pad_rows = lambda a, n: jnp.pad(a.reshape(db, n_new, -1), ((0, 0), (0, n - n_new), (0, 0)))
            o8 = _swa_sample_attn(pad_rows(qs, SUBLANES), pad_rows(ks, BF16_ROWS), pad_rows(vs, BF16_ROWS),
                                  cache_swa_k.transpose(0, 1, 3, 4, 2), cache_swa_v.transpose(0, 1, 3, 4, 2),
                                  swa_sinks[j][sink_rows].reshape(-1, 1), layer=j, n_new=n_new, bb=min(16, db))
            os_ = o8[:, :n_new].reshape(ts, nqw)
            kp4 = kp.reshape(batch, seq, SWA_KV_HEADS, HEAD_DIM)
            vp4 = vp.reshape(batch, seq, SWA_KV_HEADS, HEAD_DIM)
            outs["swa_kp"].append(kp4[:, seq - WINDOW:])
            outs["swa_vp"].append(vp4[:, seq - WINDOW:])
            new_k = ks.reshape(db, n_new, SWA_KV_HEADS, HEAD_DIM)
            new_v = vs.reshape(db, n_new, SWA_KV_HEADS, HEAD_DIM)
            outs["swa_ks"].append(jnp.concatenate([cache_swa_k[j][:, n_new:], new_k], axis=1))
            outs["swa_vs"].append(jnp.concatenate([cache_swa_v[j][:, n_new:], new_v], axis=1))
        elif kind == 1:
            w = fox_w_qkv[j].astype(BF16)
            wf = jnp.tile(fox_w_f[j], (1, LANES // FOX_HEADS)).astype(BF16)
            bf = row(jnp.tile(fox_b_f[j], LANES // FOX_HEADS))
            wo, bo = fox_w_o[j].astype(BF16), None
            kp, vp, lp, qa, ka, vt = _fox_proj_prompt(hp, g_mix, w, wf, bf, tm=min(256, seq), seq=seq)
            op = _fox_flash(qa, ka, vt, batch=batch, seq=seq, tq=min(1024, seq), tk=min(512, seq))
            outs["fox_kp"].append(kp.reshape(batch, FOX_HEADS, HEAD_DIM, seq).transpose(0, 3, 1, 2))
            outs["fox_vp"].append(vp.reshape(batch, FOX_HEADS, HEAD_DIM, seq).transpose(0, 3, 1, 2))
            outs["fox_lp"].append(lp.transpose(0, 2, 1))

            n_pool = cache_fox_k.shape[1]
            qs, ks, vs, ls = _fox_proj_sample(hs, g_mix, w, wf, bf)
            by_head = lambda a, rows: jnp.pad(a.reshape(db, n_new, FOX_HEADS, HEAD_DIM).transpose(0, 2, 1, 3),
                                              ((0, 0), (0, 0), (0, rows - n_new), (0, 0)))
            lfn = jnp.pad(ls.reshape(db, n_new, FOX_HEADS), ((0, 0), (0, BF16_ROWS - n_new), (0, 0)))
            o4 = _fox_sample_attn(cache_fox_k.transpose(0, 1, 3, 4, 2), cache_fox_v.transpose(0, 1, 3, 4, 2),
                                  cache_fox_logf.transpose(0, 1, 3, 2), page_table, by_head(qs, SUBLANES),
                                  by_head(ks, BF16_ROWS), by_head(vs, BF16_ROWS), lfn, layer=j, n_new=n_new)
            os_ = o4[:, :, :n_new].transpose(0, 2, 1, 3).reshape(ts, d)
            outs["fox_ks"].append(ks.reshape(db, n_new, FOX_HEADS, HEAD_DIM))
            outs["fox_vs"].append(vs.reshape(db, n_new, FOX_HEADS, HEAD_DIM))
            outs["fox_ls"].append(ls.reshape(db, n_new, FOX_HEADS))
        else:
            w = conv_w_in[j].astype(BF16)
            wo, bo = conv_w_out[j].astype(BF16), None
            op, up = _conv_proj(hp, g_mix, w, conv_w[j], None, tm=tm_p, seq=seq)
            st = state_conv[j]
            keep = CONV_WIDTH - 1
            s1 = jnp.pad(st[:, keep - 1:], ((0, 0), (0, n_new - 1), (0, 0))).reshape(ts, d)
            s2 = jnp.pad(st, ((0, 0), (0, n_new - keep), (0, 0))).reshape(ts, d)
            os_, us = _conv_proj(hs, g_mix, w, conv_w[j], (s1, s2), tm=tm_s, seq=n_new)
            outs["conv_p"].append(up[:, SUBLANES - keep:])
            outs["conv_s"].append(us.reshape(db, n_new, d)[:, n_new - keep:])

        wg, wu, wd = ffn_w_gate[i].astype(BF16), ffn_w_up[i].astype(BF16), ffn_w_down[i].astype(BF16)
        wpg, wpp = ple_w_gate[i].astype(BF16), ple_w_proj[i].astype(BF16)
        gn = row(norm_final) if i == depth - 1 else None
        common = dict(layer=i, ffn_chunk=_ffn_chunk(wg.shape[1]))
        hp = _post(hp, op, wo, bo, row(norm_ffn[i]), wg, wu, wd, row(norm_ple[i]), wpg,
                   p_prompt.reshape(depth, tp, -1), wpp, gn, tm=tm_p, **common)
        hs = _post(hs, os_, wo, bo, row(norm_ffn[i]), wg, wu, wd, row(norm_ple[i]), wpg,
                   p_sample.reshape(depth, ts, -1), wpp, gn, tm=tm_s, **common)

    st = lambda name: jnp.stack(outs[name])
    return (hp.reshape(batch, seq, d), hs.reshape(db, n_new, d),
            st("swa_kp"), st("swa_vp"), st("swa_ks"), st("swa_vs"),
            st("fox_kp"), st("fox_vp"), st("fox_lp"),
            st("fox_ks"), st("fox_vs"), st("fox_ls"),
            st("conv_p"), st("conv_s"))
```

```python
import functools

import numpy as np
import jax
import jax.numpy as jnp
from jax import lax
from jax.experimental import pallas as pl
from jax.experimental.pallas import tpu as pltpu

F32 = jnp.float32
BF16 = jnp.bfloat16

HEAD_DIM = 64
HALF = HEAD_DIM // 2
SWA_Q_HEADS = 16
SWA_KV_HEADS = 4
SWA_GROUP = SWA_Q_HEADS // SWA_KV_HEADS
WINDOW = 128
FOX_HEADS = 16
PAGE_SIZE = 128
PAST_LEN = 8192
N_MIXERS = 3
CONV_WIDTH = 3
ROPE_THETA = 10000.0
EPS = 1e-6
NEG = -1e30
SCALE = HEAD_DIM ** -0.5

LANES = 128
SUBLANES = 8
BF16_ROWS = 16
VMEM_LIMIT = 56 * 1024 * 1024
AUG_COL = HEAD_DIM
FOX_PAGES_PER_STEP = 16
LOG2E = 1.4426950408889634


def _params(*sem):
    return pltpu.CompilerParams(dimension_semantics=sem, vmem_limit_bytes=VMEM_LIMIT)


def _const_spec(shape):
    nd = len(shape)
    return pl.BlockSpec(shape, lambda *_: (0,) * nd, pipeline_mode=pl.Buffered(1))


def _rms(x, g):
    ms = jnp.mean(x * x, axis=-1, keepdims=True)
    return x * lax.rsqrt(ms + EPS) * g


def _dot(a, b):
    return jnp.dot(a, b, preferred_element_type=F32)


def _dot_nt(a, b):
    return lax.dot_general(a, b, (((1,), (1,)), ((), ())), preferred_element_type=F32)


def _dot_exact(a, b):
    return jnp.dot(a, b, preferred_element_type=F32, precision=lax.Precision.HIGHEST)


def _sigmoid(x):
    return 1.0 / (1.0 + jnp.exp(-x))


def _log_sigmoid(z):
    return jnp.minimum(z, 0.0) - jnp.log1p(jnp.exp(-jnp.abs(z)))


def _swa_proj_kernel(x_ref, g_ref, w_ref, b_ref, cos_ref, sin_ref, q_ref, k_ref, v_ref, vt_ref):
    xn = _rms(x_ref[...], g_ref[...]).astype(BF16)
    qkv = _dot(xn, w_ref[...]) + b_ref[...]
    nq, nk = q_ref.shape[1], k_ref.shape[1]
    cos, sin = cos_ref[...], sin_ref[...]
    lane = lax.broadcasted_iota(jnp.int32, cos.shape, 1)
    first_half = (lane % HEAD_DIM) < HALF

    def rope(xs):
        sw = jnp.where(first_half, pltpu.roll(xs, LANES - HALF, 1), pltpu.roll(xs, HALF, 1))
        return xs * cos + sw * sin

    for j in range(nq // LANES):
        q_ref[:, j * LANES:(j + 1) * LANES] = rope(qkv[:, j * LANES:(j + 1) * LANES]).astype(q_ref.dtype)
    for j in range(nk // LANES):
        k_ref[:, j * LANES:(j + 1) * LANES] = rope(qkv[:, nq + j * LANES:nq + (j + 1) * LANES])
    v_ref[...] = qkv[:, nq + nk:]
    vt_ref[...] = qkv[:, nq + nk:].T.astype(BF16)


def _swa_proj(x, g, w, b, cos_t, sin_t, *, tm, q_dtype):
    t, d = x.shape
    nq = SWA_Q_HEADS * HEAD_DIM
    nk = SWA_KV_HEADS * HEAD_DIM
    n_pos_blocks = cos_t.shape[0] // tm
    return pl.pallas_call(
        _swa_proj_kernel,
        grid=(t // tm,),
        in_specs=[
            pl.BlockSpec((tm, d), lambda i: (i, 0)),
            _const_spec((1, d)),
            _const_spec(w.shape),
            _const_spec((1, w.shape[1])),
            pl.BlockSpec((tm, LANES), lambda i: (i % n_pos_blocks, 0)),
            pl.BlockSpec((tm, LANES), lambda i: (i % n_pos_blocks, 0)),
        ],
        out_specs=[
            pl.BlockSpec((tm, nq), lambda i: (i, 0)),
            pl.BlockSpec((tm, nk), lambda i: (i, 0)),
            pl.BlockSpec((tm, nk), lambda i: (i, 0)),
            pl.BlockSpec((nk, tm), lambda i: (0, i)),
        ],
        out_shape=[
            jax.ShapeDtypeStruct((t, nq), q_dtype),
            jax.ShapeDtypeStruct((t, nk), F32),
            jax.ShapeDtypeStruct((t, nk), F32),
            jax.ShapeDtypeStruct((nk, t), BF16),
        ],
        compiler_params=_params("parallel"),
        name="swa_proj",
    )(x, g, w, b, cos_t, sin_t)


def _swa_prompt_kernel(sink_ref, q_ref, kp_ref, kc_ref, vtp_ref, vtc_ref, o_ref):
    i = pl.program_id(1)
    w = WINDOW
    kband = jnp.concatenate([kp_ref[...], kc_ref[...]], axis=0)
    vtband = jnp.concatenate([vtp_ref[...], vtc_ref[...]], axis=1)
    kj = lax.broadcasted_iota(jnp.int32, (2 * w, w), 0)
    qi = lax.broadcasted_iota(jnp.int32, (2 * w, w), 1)
    dist = qi + w - kj
    mask = (dist >= 0) & (dist < w) & ((kj >= w) | (i > 0))
    low = lax.broadcasted_iota(jnp.int32, (2 * w, LANES), 1) < HEAD_DIM
    kz = []
    for kvp in range(SWA_KV_HEADS // 2):
        kt = kband[:, kvp * LANES:(kvp + 1) * LANES]
        kz += [jnp.where(low, kt, 0.0).astype(BF16), jnp.where(low, 0.0, kt).astype(BF16)]
    pieces = []
    for g in range(SWA_GROUP):
        for kv in range(SWA_KV_HEADS):
            c0 = g * SWA_KV_HEADS * HEAD_DIM + (kv // 2) * LANES
            st = _dot_nt(kz[kv], q_ref[:, c0:c0 + LANES]) * SCALE
            pieces.append(jnp.where(mask, st, NEG))
    s = jnp.concatenate(pieces, axis=1)
    sk = sink_ref[...]
    m = jnp.maximum(jnp.max(s, axis=0, keepdims=True), sk)
    ex = jnp.exp(s - m)
    den = jnp.sum(ex, axis=0, keepdims=True) + jnp.exp(sk - m)
    p = ex.astype(BF16)
    outs = []
    for n in range(SWA_Q_HEADS):
        kv = n % SWA_KV_HEADS
        ot = _dot(vtband[kv * HEAD_DIM:(kv + 1) * HEAD_DIM, :], p[:, n * w:(n + 1) * w])
        outs.append(ot / den[:, n * w:(n + 1) * w])
    o_ref[...] = jnp.concatenate(outs, axis=0).T.astype(o_ref.dtype)


def _swa_prompt_attn(q, k, vt, sink_row, *, batch, seq):
    nb = seq // WINDOW
    nq, nk = q.shape[1], k.shape[1]
    cur = lambda b, i: (b * nb + i, 0)
    prev = lambda b, i: (b * nb + jnp.maximum(i - 1, 0), 0)
    cur_t = lambda b, i: (0, b * nb + i)
    prev_t = lambda b, i: (0, b * nb + jnp.maximum(i - 1, 0))
    return pl.pallas_call(
        _swa_prompt_kernel,
        grid=(batch, nb),
        in_specs=[
            _const_spec(sink_row.shape),
            pl.BlockSpec((WINDOW, nq), cur),
            pl.BlockSpec((WINDOW, nk), prev),
            pl.BlockSpec((WINDOW, nk), cur),
            pl.BlockSpec((nk, WINDOW), prev_t),
            pl.BlockSpec((nk, WINDOW), cur_t),
        ],
        out_specs=pl.BlockSpec((WINDOW, nq), cur),
        out_shape=jax.ShapeDtypeStruct(q.shape, BF16),
        compiler_params=_params("parallel", "parallel"),
        name="swa_prompt_attn",
    )(sink_row, q, k, k, vt, vt)


def _swa_sample_kernel(sink_ref, q_ref, kn_ref, vn_ref, ck_ref, cv_ref, o_ref, *, n_new):
    bb = q_ref.shape[0]
    buf = ck_ref.shape[4]
    rows = SWA_Q_HEADS * SUBLANES
    kvw = SWA_KV_HEADS * HEAD_DIM
    lane = lax.broadcasted_iota(jnp.int32, (SUBLANES, kvw), 1)
    head_mask = [(lane // HEAD_DIM) == kv for kv in range(SWA_KV_HEADS)]
    t = lax.broadcasted_iota(jnp.int32, (rows, 1), 0) % SUBLANES
    valid_c = lax.broadcasted_iota(jnp.int32, (rows, buf), 1) > t
    tn = lax.broadcasted_iota(jnp.int32, (rows, BF16_ROWS), 1)
    valid_n = (tn <= t) & (tn < n_new)
    sk = sink_ref[...]

    def body(b, carry):
        q8 = q_ref[b]
        pieces = []
        for kv in range(SWA_KV_HEADS):
            for g in range(SWA_GROUP):
                pieces.append(jnp.where(head_mask[kv], q8[:, g * kvw:(g + 1) * kvw], 0.0))
        qbd = jnp.concatenate(pieces, axis=0).astype(BF16)
        ckt = ck_ref[0, b].reshape(kvw, buf).astype(BF16)
        cvt = cv_ref[0, b].reshape(kvw, buf).astype(BF16)
        s_c = jnp.where(valid_c, _dot(qbd, ckt) * SCALE, NEG)
        s_n = jnp.where(valid_n, _dot_nt(qbd, kn_ref[b].astype(BF16)) * SCALE, NEG)
        m = jnp.maximum(jnp.maximum(jnp.max(s_c, axis=-1, keepdims=True),
                                    jnp.max(s_n, axis=-1, keepdims=True)), sk)
        e_c = jnp.exp(s_c - m)
        e_n = jnp.exp(s_n - m)
        den = jnp.sum(e_c, axis=-1, keepdims=True) + jnp.sum(e_n, axis=-1, keepdims=True) + jnp.exp(sk - m)
        o = (_dot_nt((e_c / den).astype(BF16), cvt)
             + _dot((e_n / den).astype(BF16), vn_ref[b].astype(BF16)))
        for g in range(SWA_GROUP):
            acc = None
            for kv in range(SWA_KV_HEADS):
                r0 = (kv * SWA_GROUP + g) * SUBLANES
                part = jnp.where(head_mask[kv], o[r0:r0 + SUBLANES], 0.0)
                acc = part if acc is None else acc + part
            o_ref[b, :, g * kvw:(g + 1) * kvw] = acc
        return carry

    lax.fori_loop(0, bb, body, 0)


def _swa_sample_attn(q8, kn16, vn16, cache_kt, cache_vt, sink_col, *, layer, n_new, bb):
    db = cache_kt.shape[1]
    kvw = cache_kt.shape[2] * cache_kt.shape[3]
    nq = q8.shape[2]
    cache_spec = pl.BlockSpec((1, bb) + cache_kt.shape[2:], lambda i: (layer, i, 0, 0, 0))
    return pl.pallas_call(
        functools.partial(_swa_sample_kernel, n_new=n_new),
        grid=(db // bb,),
        in_specs=[
            _const_spec(sink_col.shape),
            pl.BlockSpec((bb, SUBLANES, nq), lambda i: (i, 0, 0)),
            pl.BlockSpec((bb, BF16_ROWS, kvw), lambda i: (i, 0, 0)),
            pl.BlockSpec((bb, BF16_ROWS, kvw), lambda i: (i, 0, 0)),
            cache_spec,
            cache_spec,
        ],
        out_specs=pl.BlockSpec((bb, SUBLANES, nq), lambda i: (i, 0, 0)),
        out_shape=jax.ShapeDtypeStruct(q8.shape, F32),
        compiler_params=_params("parallel"),
        name="swa_sample_attn",
    )(sink_col, q8, kn16, vn16, cache_kt, cache_vt)


def _fox_logf(xn, wf_ref, bf_ref):
    return _log_sigmoid(_dot(xn, wf_ref[...]) + bf_ref[...])


def _fox_proj_prompt_kernel(x_ref, g_ref, w_ref, wf_ref, bf_ref, pq_ref, pk_ref, oq_ref, ok_ref,
                            k_ref, v_ref, lf_ref, qa_ref, ka_ref, vt_ref, carry_ref, *, tiles_per_seq):
    i = pl.program_id(0)
    tm, d = x_ref.shape

    @pl.when(i % tiles_per_seq == 0)
    def _():
        carry_ref[...] = jnp.zeros_like(carry_ref)

    xn = _rms(x_ref[...], g_ref[...]).astype(BF16)
    qkv = _dot(xn, w_ref[...])
    vt = qkv[:, 2 * d:].T
    k_ref[0] = qkv[:, d:2 * d].T
    v_ref[0] = vt
    vt_ref[...] = vt.astype(BF16)
    logf = _fox_logf(xn, wf_ref, bf_ref)
    lf_ref[0] = logf.T[:FOX_HEADS, :]

    r = lax.broadcasted_iota(jnp.int32, (tm, tm), 0)
    c = lax.broadcasted_iota(jnp.int32, (tm, tm), 1)
    ltri = jnp.where(c <= r, 1.0, 0.0).astype(F32)
    dcum = _dot_exact(ltri, logf) + carry_ref[...]
    carry_ref[...] = dcum[tm - 1:tm, :]

    d2 = dcum * LOG2E
    hi = d2.astype(BF16).astype(F32)
    r1 = d2 - hi
    mid = r1.astype(BF16).astype(F32)
    lo = r1 - mid
    grp = (lax.broadcasted_iota(jnp.int32, (tm, LANES), 1) // FOX_HEADS) % 3
    split = jnp.where(grp == 0, hi, jnp.where(grp == 1, mid, lo)).astype(BF16)
    aug_q = _dot(split, pq_ref[...]) + oq_ref[...]
    aug_k = _dot(split, pk_ref[...]) + ok_ref[...]

    low = lax.broadcasted_iota(jnp.int32, (tm, LANES), 1) < HEAD_DIM
    for h in range(FOX_HEADS):
        p, e = h // 2, h % 2
        qp = qkv[:, p * LANES:(p + 1) * LANES] * (SCALE * LOG2E)
        kp = qkv[:, d + p * LANES:d + (p + 1) * LANES]
        if e == 1:
            qp = pltpu.roll(qp, HEAD_DIM, 1)
            kp = pltpu.roll(kp, HEAD_DIM, 1)
        qa_ref[:, h * LANES:(h + 1) * LANES] = jnp.where(low, qp, aug_q[:, h * LANES:(h + 1) * LANES]).astype(BF16)
        ka_ref[:, h * LANES:(h + 1) * LANES] = jnp.where(low, kp, aug_k[:, h * LANES:(h + 1) * LANES]).astype(BF16)


def _fox_aug_constants():
    pq = np.zeros((LANES, FOX_HEADS * LANES), np.float32)
    pk = np.zeros((LANES, FOX_HEADS * LANES), np.float32)
    oq = np.zeros((1, FOX_HEADS * LANES), np.float32)
    ok = np.zeros((1, FOX_HEADS * LANES), np.float32)
    for h in range(FOX_HEADS):
        base = h * LANES + AUG_COL
        for c in range(3):
            pk[c * FOX_HEADS + h, base + c] = -1.0
            oq[0, base + c] = 1.0
            pq[c * FOX_HEADS + h, base + 3 + c] = 1.0
            ok[0, base + 3 + c] = 1.0
    return jnp.asarray(pq, BF16), jnp.asarray(pk, BF16), jnp.asarray(oq), jnp.asarray(ok)


def _fox_proj_prompt(x, g, w, wf, bf, *, tm, seq):
    t, d = x.shape
    pq, pk, oq, ok = _fox_aug_constants()
    aw = FOX_HEADS * LANES
    tiles_per_seq = seq // tm
    row = lambda i: (i, 0)
    by_seq = lambda i: (i // tiles_per_seq, 0, i % tiles_per_seq)
    return pl.pallas_call(
        functools.partial(_fox_proj_prompt_kernel, tiles_per_seq=tiles_per_seq),
        grid=(t // tm,),
        in_specs=[
            pl.BlockSpec((tm, d), row),
            _const_spec((1, d)),
            _const_spec(w.shape),
            _const_spec(wf.shape),
            _const_spec(bf.shape),
            _const_spec(pq.shape),
            _const_spec(pk.shape),
            _const_spec(oq.shape),
            _const_spec(ok.shape),
        ],
        out_specs=[
            pl.BlockSpec((1, d, tm), by_seq),
            pl.BlockSpec((1, d, tm), by_seq),
            pl.BlockSpec((1, FOX_HEADS, tm), by_seq),
            pl.BlockSpec((tm, aw), row),
            pl.BlockSpec((tm, aw), row),
            pl.BlockSpec((d, tm), lambda i: (0, i)),
        ],
        out_shape=[
            jax.ShapeDtypeStruct((t // seq, d, seq), F32),
            jax.ShapeDtypeStruct((t // seq, d, seq), F32),
            jax.ShapeDtypeStruct((t // seq, FOX_HEADS, seq), F32),
            jax.ShapeDtypeStruct((t, aw), BF16),
            jax.ShapeDtypeStruct((t, aw), BF16),
            jax.ShapeDtypeStruct((d, t), BF16),
        ],
        scratch_shapes=[pltpu.VMEM((1, LANES), F32)],
        compiler_params=_params("arbitrary"),
        name="fox_proj_prompt",
    )(x, g, w, wf, bf, pq, pk, oq, ok)


def _fox_proj_sample_kernel(x_ref, g_ref, w_ref, wf_ref, bf_ref, q_ref, k_ref, v_ref, lf_ref):
    d = k_ref.shape[1]
    xn = _rms(x_ref[...], g_ref[...]).astype(BF16)
    qkv = _dot(xn, w_ref[...])
    q_ref[...] = qkv[:, :d]
    k_ref[...] = qkv[:, d:2 * d]
    v_ref[...] = qkv[:, 2 * d:]
    lf_ref[...] = _fox_logf(xn, wf_ref, bf_ref)[:, :FOX_HEADS]


def _fox_proj_sample(x, g, w, wf, bf):
    t, d = x.shape
    return pl.pallas_call(
        _fox_proj_sample_kernel,
        grid=(1,),
        in_specs=[
            _const_spec((t, d)),
            _const_spec((1, d)),
            _const_spec(w.shape),
            _const_spec(wf.shape),
            _const_spec(bf.shape),
        ],
        out_specs=[
            _const_spec((t, d)), _const_spec((t, d)), _const_spec((t, d)),
            _const_spec((t, FOX_HEADS)),
        ],
        out_shape=[
            jax.ShapeDtypeStruct((t, d), F32),
            jax.ShapeDtypeStruct((t, d), F32),
            jax.ShapeDtypeStruct((t, d), F32),
            jax.ShapeDtypeStruct((t, FOX_HEADS), F32),
        ],
        compiler_params=_params("arbitrary"),
        name="fox_proj_sample",
    )(x, g, w, wf, bf)


def _fox_flash_kernel(qi_ref, kj_ref, q_ref, k_ref, vt_ref, o_ref, m_ref, l_ref, acc_ref):
    step = pl.program_id(2)
    i, j = qi_ref[step], kj_ref[step]
    tq, tk = q_ref.shape[0], k_ref.shape[0]

    @pl.when(j == 0)
    def _():
        m_ref[...] = jnp.full_like(m_ref, NEG)
        l_ref[...] = jnp.zeros_like(l_ref)
        acc_ref[...] = jnp.zeros_like(acc_ref)

    def update(masked):
        for e in range(4):
            st = _dot_nt(k_ref[:, e * LANES:(e + 1) * LANES], q_ref[:, e * LANES:(e + 1) * LANES])
            if masked:
                key = j * tk + lax.broadcasted_iota(jnp.int32, (tk, tq), 0)
                qry = i * tq + lax.broadcasted_iota(jnp.int32, (tk, tq), 1)
                st = jnp.where(key > qry, NEG, st)
            m_prev = m_ref[e]
            m_new = jnp.maximum(m_prev, jnp.max(st, axis=0, keepdims=True))
            alpha = jnp.exp2(m_prev - m_new)
            p = jnp.exp2(st - m_new)
            l_ref[e] = alpha * l_ref[e] + jnp.sum(p, axis=0, keepdims=True)
            m_ref[e] = m_new
            rows = slice(e * HEAD_DIM, (e + 1) * HEAD_DIM)
            acc_ref[rows, :] = acc_ref[rows, :] * alpha + _dot(vt_ref[rows, :], p.astype(BF16))

    crosses_diagonal = (j + 1) * tk - 1 > i * tq
    pl.when(crosses_diagonal)(functools.partial(update, True))
    pl.when(jnp.logical_not(crosses_diagonal))(functools.partial(update, False))

    @pl.when(j == ((i + 1) * tq - 1) // tk)
    def _():
        ot = jnp.concatenate([acc_ref[e * HEAD_DIM:(e + 1) * HEAD_DIM, :] / l_ref[e] for e in range(4)], axis=0)
        o_ref[...] = ot.T.astype(o_ref.dtype)


def _fox_flash(qa, ka, vt, *, batch, seq, tq, tk):
    nq, nk = seq // tq, seq // tk
    pairs = [(i, j) for i in range(nq) for j in range(((i + 1) * tq - 1) // tk + 1)]
    qi = jnp.asarray([p[0] for p in pairs], jnp.int32)
    kj = jnp.asarray([p[1] for p in pairs], jnp.int32)
    d, t = vt.shape
    grid_spec = pltpu.PrefetchScalarGridSpec(
        num_scalar_prefetch=2,
        grid=(batch, FOX_HEADS // 4, len(pairs)),
        in_specs=[
            pl.BlockSpec((tq, 4 * LANES), lambda b, p, s, qi, kj: (b * nq + qi[s], p)),
            pl.BlockSpec((tk, 4 * LANES), lambda b, p, s, qi, kj: (b * nk + kj[s], p)),
            pl.BlockSpec((2 * LANES, tk), lambda b, p, s, qi, kj: (p, b * nk + kj[s])),
        ],
        out_specs=pl.BlockSpec((tq, 2 * LANES), lambda b, p, s, qi, kj: (b * nq + qi[s], p)),
        scratch_shapes=[
            pltpu.VMEM((4, 1, tq), F32),
            pltpu.VMEM((4, 1, tq), F32),
            pltpu.VMEM((2 * LANES, tq), F32),
        ],
    )
    return pl.pallas_call(
        _fox_flash_kernel,
        grid_spec=grid_spec,
        out_shape=jax.ShapeDtypeStruct((t, d), BF16),
        compiler_params=_params("parallel", "parallel", "arbitrary"),
        name="fox_flash",
    )(qi, kj, qa, ka, vt)


def _fox_sample_kernel(pt_ref, *refs, n_new):
    n = FOX_PAGES_PER_STEP
    k_refs, v_refs, lf_refs = refs[:n], refs[n:2 * n], refs[2 * n:3 * n]
    q_ref, kn_ref, vn_ref, lfn_ref, o_ref, m_ref, l_ref, acc_ref, carry_ref = refs[3 * n:]
    j = pl.program_id(1)

    @pl.when(j == 0)
    def _():
        m_ref[...] = jnp.full_like(m_ref, NEG)
        l_ref[...] = jnp.zeros_like(l_ref)
        acc_ref[...] = jnp.zeros_like(acc_ref)
        carry_ref[...] = jnp.zeros_like(carry_ref)

    ui = lax.broadcasted_iota(jnp.int32, (PAGE_SIZE, PAGE_SIZE), 0)
    uj = lax.broadcasted_iota(jnp.int32, (PAGE_SIZE, PAGE_SIZE), 1)
    upper = jnp.where(ui <= uj, 1.0, 0.0).astype(F32)
    within = _dot_exact(jnp.concatenate([r[0, 0] for r in lf_refs], axis=0), upper)
    carry = carry_ref[...]
    parts = []
    for pg in range(n):
        dpg = within[pg * FOX_HEADS:(pg + 1) * FOX_HEADS] + carry
        carry = dpg[:, PAGE_SIZE - 1:PAGE_SIZE]
        parts.append(dpg)
    carry_ref[...] = carry
    dpast = jnp.concatenate(parts, axis=1)

    def keys_on_lanes(page_refs, h):
        return jnp.concatenate([r[0, 0, h] for r in page_refs], axis=1).astype(BF16)

    heads = range(FOX_HEADS)
    q = [(q_ref[0, h] * SCALE).astype(BF16) for h in heads]

    def per_head_rows(x):
        return jnp.concatenate([jnp.broadcast_to(x[h:h + 1, :], (SUBLANES, x.shape[1])) for h in heads], axis=0)

    def head_block(x, h):
        return x[h * SUBLANES:(h + 1) * SUBLANES]

    s = jnp.concatenate([_dot(q[h], keys_on_lanes(k_refs, h)) for h in heads], axis=0) - per_head_rows(dpast)
    m_prev = m_ref[...]
    m_new = jnp.maximum(m_prev, jnp.max(s, axis=-1, keepdims=True))
    alpha = jnp.exp(m_prev - m_new)
    p = jnp.exp(s - m_new)
    l_ref[...] = alpha * l_ref[...] + jnp.sum(p, axis=-1, keepdims=True)
    m_ref[...] = m_new
    pv = jnp.concatenate([_dot_nt(head_block(p, h).astype(BF16), keys_on_lanes(v_refs, h)) for h in heads], axis=0)
    acc_ref[...] = acc_ref[...] * alpha + pv

    @pl.when(j == pl.num_programs(1) - 1)
    def _():
        rows = lfn_ref.shape[1]
        ti = lax.broadcasted_iota(jnp.int32, (rows, rows), 0)
        tj = lax.broadcasted_iota(jnp.int32, (rows, rows), 1)
        last_key = jnp.where(lax.broadcasted_iota(jnp.int32, (rows, PAGE_SIZE), 1) == PAGE_SIZE - 1, 1.0, 0.0)
        d_last = lax.dot_general(last_key.astype(F32), parts[-1], (((1,), (1,)), ((), ())),
                                 precision=lax.Precision.HIGHEST, preferred_element_type=F32)
        d_new = d_last + _dot_exact(jnp.where(tj <= ti, 1.0, 0.0).astype(F32), lfn_ref[0])
        d_new_t = lax.dot_general(jnp.where(ti == tj, 1.0, 0.0).astype(F32), d_new, (((1,), (1,)), ((), ())),
                                  precision=lax.Precision.HIGHEST, preferred_element_type=F32)
        dq = jnp.concatenate([d_new[:SUBLANES, h:h + 1] for h in heads], axis=0)
        s_n = (jnp.concatenate([_dot_nt(q[h], kn_ref[0, h].astype(BF16)) for h in heads], axis=0)
               - per_head_rows(d_new_t) + dq)
        t_q = lax.broadcasted_iota(jnp.int32, s_n.shape, 0) % SUBLANES
        t_k = lax.broadcasted_iota(jnp.int32, s_n.shape, 1)
        s_n = jnp.where((t_k <= t_q) & (t_k < n_new), s_n, NEG)
        m_old = m_ref[...] + dq
        m_fin = jnp.maximum(m_old, jnp.max(s_n, axis=-1, keepdims=True))
        a_fin = jnp.exp(m_old - m_fin)
        p_n = jnp.exp(s_n - m_fin)
        l_fin = a_fin * l_ref[...] + jnp.sum(p_n, axis=-1, keepdims=True)
        pv_n = jnp.concatenate([_dot(head_block(p_n, h).astype(BF16), vn_ref[0, h].astype(BF16)) for h in heads],
                               axis=0)
        o = (acc_ref[...] * a_fin + pv_n) / l_fin
        for h in heads:
            o_ref[0, h] = head_block(o, h)


def _fox_sample_attn(kt_pages, vt_pages, lt_pages, page_table, q4, kn4, vn4, lfn, *, layer, n_new):
    db, n_pages = page_table.shape
    n = FOX_PAGES_PER_STEP
    steps = n_pages // n

    def pool_spec(k, arr):
        nd = arr.ndim - 2
        return pl.BlockSpec((1, 1) + arr.shape[2:], lambda b, j, pt: (layer, pt[b, j * n + k]) + (0,) * nd)

    def per_seq(arr):
        nd = arr.ndim - 1
        return pl.BlockSpec((1,) + arr.shape[1:], lambda b, j, pt: (b,) + (0,) * nd)

    grid_spec = pltpu.PrefetchScalarGridSpec(
        num_scalar_prefetch=1,
        grid=(db, steps),
        in_specs=([pool_spec(k, kt_pages) for k in range(n)] + [pool_spec(k, vt_pages) for k in range(n)]
                  + [pool_spec(k, lt_pages) for k in range(n)]
                  + [per_seq(q4), per_seq(kn4), per_seq(vn4), per_seq(lfn)]),
        out_specs=per_seq(q4),
        scratch_shapes=[
            pltpu.VMEM((FOX_HEADS * SUBLANES, 1), F32),
            pltpu.VMEM((FOX_HEADS * SUBLANES, 1), F32),
            pltpu.VMEM((FOX_HEADS * SUBLANES, HEAD_DIM), F32),
            pltpu.VMEM((FOX_HEADS, 1), F32),
        ],
    )
    return pl.pallas_call(
        functools.partial(_fox_sample_kernel, n_new=n_new),
        grid_spec=grid_spec,
        out_shape=jax.ShapeDtypeStruct(q4.shape, F32),
        compiler_params=_params("parallel", "arbitrary"),
        name="fox_sample_attn",
    )(page_table, *([kt_pages] * n), *([vt_pages] * n), *([lt_pages] * n), q4, kn4, vn4, lfn)


def _conv_proj_kernel(*refs, seq, has_state):
    if has_state:
        x_ref, g_ref, w_ref, cw_ref, s1_ref, s2_ref, y_ref, u_ref, carry_ref = refs
    else:
        x_ref, g_ref, w_ref, cw_ref, y_ref, u_ref, carry_ref = refs
    i = pl.program_id(0)
    tm, d = x_ref.shape

    @pl.when(i == 0)
    def _():
        carry_ref[...] = jnp.zeros_like(carry_ref)

    xn = _rms(x_ref[...], g_ref[...]).astype(BF16)
    proj = _dot(xn, w_ref[...])
    gb, gc, hh = proj[:, :d], proj[:, d:2 * d], proj[:, 2 * d:]
    u = gc * hh
    row = lax.broadcasted_iota(jnp.int32, (tm, 1), 0)
    pos = (i * tm + row) % seq
    prev_tile = carry_ref[...]
    um1 = jnp.where(row == 0, prev_tile[7:8, :], pltpu.roll(u, 1, 0))
    um2 = jnp.where(row == 0, prev_tile[6:7, :], jnp.where(row == 1, prev_tile[7:8, :], pltpu.roll(u, 2, 0)))
    um1 = jnp.where(pos >= 1, um1, 0.0)
    um2 = jnp.where(pos >= 2, um2, 0.0)
    if has_state:
        um1 = um1 + s1_ref[...]
        um2 = um2 + s2_ref[...]
    cw = cw_ref[...]
    conv = cw[0:1, :] * um2 + cw[1:2, :] * um1 + cw[2:3, :] * u
    y_ref[...] = (gb * conv).astype(y_ref.dtype)
    carry_ref[...] = u[tm - SUBLANES:, :]
    if has_state:
        u_ref[...] = u
    else:
        u_ref[0] = u[tm - SUBLANES:, :]


def _conv_proj(x, g, w, cw, state_rows, *, tm, seq):
    t, d = x.shape
    has_state = state_rows is not None
    row = lambda i: (i, 0)
    in_specs = [pl.BlockSpec((tm, d), row), _const_spec((1, d)), _const_spec(w.shape), _const_spec(cw.shape)]
    args = [x, g, w, cw]
    if has_state:
        in_specs += [pl.BlockSpec((tm, d), row), pl.BlockSpec((tm, d), row)]
        args += list(state_rows)
        u_spec = pl.BlockSpec((tm, d), row)
        u_shape = jax.ShapeDtypeStruct((t, d), F32)
    else:
        tiles_per_seq = seq // tm
        u_spec = pl.BlockSpec((1, SUBLANES, d), lambda i: (i // tiles_per_seq, 0, 0))
        u_shape = jax.ShapeDtypeStruct((t // seq, SUBLANES, d), F32)
    return pl.pallas_call(
        functools.partial(_conv_proj_kernel, seq=seq, has_state=has_state),
        grid=(t // tm,),
        in_specs=in_specs,
        out_specs=[pl.BlockSpec((tm, d), row), u_spec],
        out_shape=[jax.ShapeDtypeStruct((t, d), BF16), u_shape],
        scratch_shapes=[pltpu.VMEM((SUBLANES, d), F32)],
        compiler_params=_params("arbitrary"),
        name="conv_proj",
    )(*args)


def _post_kernel(*refs, has_bias, final_norm, ffn_chunk):
    refs = list(refs)
    h_ref, o_ref, wo_ref = refs[:3]
    refs = refs[3:]
    bo_ref = refs.pop(0) if has_bias else None
    gf_ref, wg_ref, wu_ref, wd_ref, gp_ref, wpg_ref, p_ref, wpp_ref = refs[:8]
    refs = refs[8:]
    gn_ref = refs.pop(0) if final_norm else None
    out_ref, act_ref = refs

    y = _dot(o_ref[...].astype(BF16), wo_ref[...])
    if has_bias:
        y = y + bo_ref[...]
    h1 = h_ref[...] + y
    hn = _rms(h1, gf_ref[...]).astype(BF16)
    ffn = wg_ref.shape[1]
    for c in range(ffn // ffn_chunk):
        sl = slice(c * ffn_chunk, (c + 1) * ffn_chunk)
        gate = _dot(hn, wg_ref[:, sl])
        up = _dot(hn, wu_ref[:, sl])
        act_ref[:, sl] = (gate * _sigmoid(gate) * up).astype(BF16)
    h2 = h1 + _dot(act_ref[...], wd_ref[...])
    pg = _sigmoid(_dot(_rms(h2, gp_ref[...]).astype(BF16), wpg_ref[...]))
    h3 = h2 + pg * _dot(p_ref[0].astype(BF16), wpp_ref[...])
    if final_norm:
        h3 = _rms(h3, gn_ref[...])
    out_ref[...] = h3


def _post(h, o, wo, bo, gf, wg, wu, wd, gp, wpg, p, wpp, gn, *, layer, tm, ffn_chunk):
    t, d = h.shape
    row = lambda i: (i, 0)
    in_specs = [pl.BlockSpec((tm, d), row), pl.BlockSpec((tm, o.shape[1]), row), _const_spec(wo.shape)]
    args = [h, o, wo]
    if bo is not None:
        in_specs.append(_const_spec(bo.shape))
        args.append(bo)
    in_specs += [_const_spec(gf.shape), _const_spec(wg.shape), _const_spec(wu.shape), _const_spec(wd.shape),
                 _const_spec(gp.shape), _const_spec(wpg.shape),
                 pl.BlockSpec((1, tm, p.shape[2]), lambda i: (layer, i, 0)),
                 _const_spec(wpp.shape)]
    args += [gf, wg, wu, wd, gp, wpg, p, wpp]
    if gn is not None:
        in_specs.append(_const_spec(gn.shape))
        args.append(gn)
    return pl.pallas_call(
        functools.partial(_post_kernel, has_bias=bo is not None, final_norm=gn is not None, ffn_chunk=ffn_chunk),
        grid=(t // tm,),
        in_specs=in_specs,
        out_specs=pl.BlockSpec((tm, d), row),
        out_shape=jax.ShapeDtypeStruct((t, d), F32),
        scratch_shapes=[pltpu.VMEM((tm, wg.shape[1]), BF16)],
        compiler_params=_params("parallel"),
        name="post_mixer",
    )(*args)


def _rope_tables(pos):
    inv = ROPE_THETA ** (-jnp.arange(HALF, dtype=F32) / HALF)
    ang = pos.astype(F32)[:, None] * inv[None, :]
    cos, sin = jnp.cos(ang), jnp.sin(ang)
    reps = LANES // HEAD_DIM
    return jnp.tile(cos, (1, 2 * reps)), jnp.tile(jnp.concatenate([-sin, sin], axis=1), (1, reps))


def _ffn_chunk(ffn):
    for c in (512, 256, 128):
        if ffn % c == 0:
            return c
    return ffn


def kernel(x_prompt, x_sample, cache_swa_k, cache_swa_v, cache_fox_k, cache_fox_v, cache_fox_logf, state_conv, page_table, p_prompt, p_sample, norm_mix, norm_ffn, norm_ple, norm_final, swa_w_qkv, swa_b_qkv, swa_w_o, swa_b_o, swa_sinks, fox_w_qkv, fox_w_f, fox_b_f, fox_w_o, conv_w_in, conv_w, conv_w_out, ffn_w_gate, ffn_w_up, ffn_w_down, ple_w_gate, ple_w_proj):
    batch, seq, d = x_prompt.shape
    db, n_new, _ = x_sample.shape
    depth = norm_mix.shape[0]
    tp, ts = batch * seq, db * n_new
    assert d == FOX_HEADS * HEAD_DIM == SWA_Q_HEADS * HEAD_DIM
    assert cache_swa_k.shape[2] == WINDOW and seq % WINDOW == 0 and n_new <= SUBLANES
    tm_p = min(512, seq)
    tm_s = ts

    hp = x_prompt.reshape(tp, d)
    hs = x_sample.reshape(ts, d)
    row = lambda a: a.reshape(1, -1)

    cos_p, sin_p = _rope_tables(jnp.arange(seq))
    cos_s, sin_s = _rope_tables(jnp.tile(PAST_LEN + jnp.arange(n_new), db))

    nqw = SWA_Q_HEADS * HEAD_DIM
    nkw = SWA_KV_HEADS * HEAD_DIM
    qperm = np.arange(nqw).reshape(SWA_KV_HEADS, SWA_GROUP, HEAD_DIM).transpose(1, 0, 2).reshape(-1)
    qkv_perm = np.concatenate([qperm, np.arange(nqw, nqw + 2 * nkw)])
    sink_rows = np.repeat(np.arange(SWA_Q_HEADS), SUBLANES)
    sink_heads = np.arange(SWA_Q_HEADS).reshape(SWA_KV_HEADS, SWA_GROUP).T.reshape(-1)

    outs = {name: [] for name in ("swa_kp", "swa_vp", "swa_ks", "swa_vs", "fox_kp", "fox_vp", "fox_lp",
                                  "fox_ks", "fox_vs", "fox_ls", "conv_p", "conv_s")}
    for i in range(depth):
        j, kind = i // N_MIXERS, i % N_MIXERS
        g_mix = row(norm_mix[i])
        if kind == 0:
            w = swa_w_qkv[j][:, qkv_perm].astype(BF16)
            b = row(swa_b_qkv[j][qkv_perm])
            wo = swa_w_o[j][qperm, :].astype(BF16)
            bo = row(swa_b_o[j])
            qp, kp, vp, vtp = _swa_proj(hp, g_mix, w, b, cos_p, sin_p, tm=tm_p, q_dtype=BF16)
            sink_row = jnp.repeat(swa_sinks[j][sink_heads], WINDOW).reshape(1, -1)
            op = _swa_prompt_attn(qp, kp, vtp, sink_row, batch=batch, seq=seq)
            qs, ks, vs, _ = _swa_proj(hs, g_mix, w, b, cos_s, sin_s, tm=tm_s, q_dtype=F32)
            pad_rows = lambda a, n: jnp.pad(a.reshape(db, n_new, -1), ((0, 0), (0, n - n_new), (0, 0)))
            o8 = _swa_sample_attn(pad_rows(qs, SUBLANES), pad_rows(ks, BF16_ROWS), pad_rows(vs, BF16_ROWS),
                                  cache_swa_k.transpose(0, 1, 3, 4, 2), cache_swa_v.transpose(0, 1, 3, 4, 2),
                                  swa_sinks[j][sink_rows].reshape(-1, 1), layer=j, n_new=n_new, bb=min(16, db))
            os_ = o8[:, :n_new].reshape(ts, nqw)
            kp4 = kp.reshape(batch, seq, SWA_KV_HEADS, HEAD_DIM)
            vp4 = vp.reshape(batch, seq, SWA_KV_HEADS, HEAD_DIM)
            outs["swa_kp"].append(kp4[:, seq - WINDOW:])
            outs["swa_vp"].append(vp4[:, seq - WINDOW:])
            new_k = ks.reshape(db, n_new, SWA_KV_HEADS, HEAD_DIM)
            new_v = vs.reshape(db, n_new, SWA_KV_HEADS, HEAD_DIM)
            outs["swa_ks"].append(jnp.concatenate([cache_swa_k[j][:, n_new:], new_k], axis=1))
            outs["swa_vs"].append(jnp.concatenate([cache_swa_v[j][:, n_new:], new_v], axis=1))
        elif kind == 1:
            w = fox_w_qkv[j].astype(BF16)
            wf = jnp.tile(fox_w_f[j], (1, LANES // FOX_HEADS)).astype(BF16)
            bf = row(jnp.tile(fox_b_f[j], LANES // FOX_HEADS))
            wo, bo = fox_w_o[j].astype(BF16), None
            kp, vp, lp, qa, ka, vt = _fox_proj_prompt(hp, g_mix, w, wf, bf, tm=min(256, seq), seq=seq)
            op = _fox_flash(qa, ka, vt, batch=batch, seq=seq, tq=min(1024, seq), tk=min(512, seq))
            outs["fox_kp"].append(kp.reshape(batch, FOX_HEADS, HEAD_DIM, seq).transpose(0, 3, 1, 2))
            outs["fox_vp"].append(vp.reshape(batch, FOX_HEADS, HEAD_DIM, seq).transpose(0, 3, 1, 2))
            outs["fox_lp"].append(lp.transpose(0, 2, 1))

            n_pool = cache_fox_k.shape[1]
            qs, ks, vs, ls = _fox_proj_sample(hs, g_mix, w, wf, bf)
            by_head = lambda a, rows: jnp.pad(a.reshape(db, n_new, FOX_HEADS, HEAD_DIM).transpose(0, 2, 1, 3),
                                              ((0, 0), (0, 0), (0, rows - n_new), (0, 0)))
            lfn = jnp.pad(ls.reshape(db, n_new, FOX_HEADS), ((0, 0), (0, BF16_ROWS - n_new), (0, 0)))
            o4 = _fox_sample_attn(cache_fox_k.transpose(0, 1, 3, 4, 2), cache_fox_v.transpose(0, 1, 3, 4, 2),
                                  cache_fox_logf.transpose(0, 1, 3, 2), page_table, by_head(qs, SUBLANES),
                                  by_head(ks, BF16_ROWS), by_head(vs, BF16_ROWS), lfn, layer=j, n_new=n_new)
            os_ = o4[:, :, :n_new].transpose(0, 2, 1, 3).reshape(ts, d)
            outs["fox_ks"].append(ks.reshape(db, n_new, FOX_HEADS, HEAD_DIM))
            outs["fox_vs"].append(vs.reshape(db, n_new, FOX_HEADS, HEAD_DIM))
            outs["fox_ls"].append(ls.reshape(db, n_new, FOX_HEADS))
        else:
            w = conv_w_in[j].astype(BF16)
            wo, bo = conv_w_out[j].astype(BF16), None
            op, up = _conv_proj(hp, g_mix, w, conv_w[j], None, tm=tm_p, seq=seq)
            st = state_conv[j]
            keep = CONV_WIDTH - 1
            s1 = jnp.pad(st[:, keep - 1:], ((0, 0), (0, n_new - 1), (0, 0))).reshape(ts, d)
            s2 = jnp.pad(st, ((0, 0), (0, n_new - keep), (0, 0))).reshape(ts, d)
            os_, us = _conv_proj(hs, g_mix, w, conv_w[j], (s1, s2), tm=tm_s, seq=n_new)
            outs["conv_p"].append(up[:, SUBLANES - keep:])
            outs["conv_s"].append(us.reshape(db, n_new, d)[:, n_new - keep:])

        wg, wu, wd = ffn_w_gate[i].astype(BF16), ffn_w_up[i].astype(BF16), ffn_w_down[i].astype(BF16)
        wpg, wpp = ple_w_gate[i].astype(BF16), ple_w_proj[i].astype(BF16)
        gn = row(norm_final) if i == depth - 1 else None
        common = dict(layer=i, ffn_chunk=_ffn_chunk(wg.shape[1]))
        hp = _post(hp, op, wo, bo, row(norm_ffn[i]), wg, wu, wd, row(norm_ple[i]), wpg,
                   p_prompt.reshape(depth, tp, -1), wpp, gn, tm=tm_p, **common)
        hs = _post(hs, os_, wo, bo, row(norm_ffn[i]), wg, wu, wd, row(norm_ple[i]), wpg,
                   p_sample.reshape(depth, ts, -1), wpp, gn, tm=tm_s, **common)

    st = lambda name: jnp.stack(outs[name])
    return (hp.reshape(batch, seq, d), hs.reshape(db, n_new, d),
            st("swa_kp"), st("swa_vp"), st("swa_ks"), st("swa_vs"),
            st("fox_kp"), st("fox_vp"), st("fox_lp"),
            st("fox_ks"), st("fox_vs"), st("fox_ls"),
            st("conv_p"), st("conv_s"))
```
